```python
import jax, jax.numpy as jnp
from jax import lax
import numpy as np

D_MODEL = 1024
BATCH = 2
SEQ = 8192
DEPTH = 1

MIX_WIDTH = D_MODEL
HGRN_WIDTH = MIX_WIDTH // 2
CONV_WIDTH = MIX_WIDTH - HGRN_WIDTH
HGRN_HEAD_DIM = 128
HGRN_HEADS = HGRN_WIDTH // HGRN_HEAD_DIM
CONV_GROUPS = 4
CONV_GROUP_DIM = CONV_WIDTH // CONV_GROUPS
CONV_TAPS = 3
CHUNK = 64
IN_COLS = 4 * HGRN_WIDTH + 3 * CONV_WIDTH
PEER_HEADS = 8
PEER_NKEYS = 128
PEER_EXPERTS = PEER_NKEYS * PEER_NKEYS
PEER_QDIM = 256
PEER_HALF = PEER_QDIM // 2
PEER_TOPK = 16
TOKEN_BLOCK = 128
EPS = 1e-6

kernel_name = "hymba_hgrn2_shortconv_peer_adaln"


def rmsnorm(x, w):
    xf = x.astype(jnp.float32)
    y = xf * lax.rsqrt(jnp.mean(xf * xf, axis=-1, keepdims=True) + EPS)
    return (y * w.astype(jnp.float32)).astype(x.dtype)


def group_rmsnorm(x, w, groups):
    shp = x.shape
    xg = x.reshape(shp[:-1] + (groups, shp[-1] // groups))
    return rmsnorm(xg, w.reshape(groups, shp[-1] // groups)).reshape(shp)


def hgrn2_chunkwise(q, k, v, log_f):
    B, H, S, DK = q.shape
    DV = v.shape[-1]
    n = S // CHUNK

    def to_chunks(t):
        return t.reshape(B, H, n, CHUNK, t.shape[-1]).transpose(2, 0, 1, 3, 4)

    qc, kc, vc, gc = to_chunks(q), to_chunks(k), to_chunks(v), to_chunks(log_f)
    causal = jnp.tril(jnp.ones((CHUNK, CHUNK), dtype=bool))[:, :, None]

    def step(state, inp):
        qb, kb, vb, gb = inp
        b = jnp.cumsum(gb, axis=2)
        diff = b[:, :, :, None, :] - b[:, :, None, :, :]
        decay = jnp.exp(jnp.where(causal, diff, -jnp.inf))
        scores = jnp.einsum('bhtd,bhsd,bhtsd->bhts', qb, kb, decay)
        o = (jnp.einsum('bhts,bhsv->bhtv', scores, vb)
             + jnp.einsum('bhtd,bhdv->bhtv', qb * jnp.exp(b), state))
        b_last = b[:, :, -1:, :]
        new_state = (jnp.exp(b_last[:, :, 0, :])[..., None] * state
                     + jnp.einsum('bhsd,bhsv->bhdv', kb * jnp.exp(b_last - b), vb))
        return new_state, o

    s0 = jnp.zeros((B, H, DK, DV), jnp.float32)
    _, o = lax.scan(step, s0, (qc, kc, vc, gc))
    return o.transpose(1, 2, 0, 3, 4).reshape(B, H, S, DV)


def hgrn2_group(p, lb, onorm_w):
    B, S, _ = p.shape
    pf = p.astype(jnp.float32)
    q, f_raw, i, g = jnp.split(pf, 4, axis=-1)
    f = lb + (1.0 - lb) * jax.nn.sigmoid(f_raw)
    log_f = jnp.log(f)
    k = (1.0 - lb) * jax.nn.sigmoid(-f_raw)

    def heads(t):
        return t.reshape(B, S, HGRN_HEADS, HGRN_HEAD_DIM).transpose(0, 2, 1, 3)

    o = hgrn2_chunkwise(heads(q), heads(k), heads(i), heads(log_f))
    o = o.transpose(0, 2, 1, 3).reshape(B, S, HGRN_WIDTH)
    o = group_rmsnorm(o, onorm_w, HGRN_HEADS) * jax.nn.silu(g)
    return o.astype(p.dtype)


def short_conv_group(p, conv_w, onorm_w):
    b_gate, c_gate, xin = jnp.split(p, 3, axis=-1)
    u = c_gate * xin
    y = lax.conv_general_dilated(
        u, conv_w[:, None, :].astype(u.dtype), window_strides=(1,),
        padding=[(CONV_TAPS - 1, 0)], dimension_numbers=('NWC', 'WIO', 'NWC'),
        feature_group_count=CONV_WIDTH)
    y = b_gate * y
    return group_rmsnorm(y, onorm_w, CONV_GROUPS)


def peer(h, w_query, sub_keys, expert_u, expert_v):
    B, S, D = h.shape
    hb = h.reshape(B * S // TOKEN_BLOCK, TOKEN_BLOCK, D)

    def block(xb):
        T = xb.shape[0]
        q = (xb @ w_query).reshape(T, PEER_HEADS, 2, PEER_HALF)
        s = jnp.einsum('thpk,hpnk->thpn', q, sub_keys).astype(jnp.float32)
        top_s, top_i = lax.top_k(s, PEER_TOPK)
        cand_s = top_s[:, :, 0, :, None] + top_s[:, :, 1, None, :]
        cand_i = top_i[:, :, 0, :, None] * PEER_NKEYS + top_i[:, :, 1, None, :]
        best_s, best_pos = lax.top_k(cand_s.reshape(T, PEER_HEADS, PEER_TOPK * PEER_TOPK), PEER_TOPK)
        idx = jnp.take_along_axis(cand_i.reshape(T, PEER_HEADS, PEER_TOPK * PEER_TOPK), best_pos, axis=-1)
        gate = jax.nn.softmax(best_s, axis=-1)
        u = expert_u[idx]
        v = expert_v[idx]
        act = jax.nn.gelu(jnp.einsum('td,thkd->thk', xb, u).astype(jnp.float32), approximate=False)
        return jnp.einsum('thk,thkd->td', (gate * act).astype(xb.dtype), v)

    return lax.map(block, hb).reshape(B, S, D)


def setup_inputs(seed: int = 0) -> dict:
    key = jax.random.key(seed)
    ks = jax.random.split(key, 17)
    f32 = jnp.float32
    D = D_MODEL
    return {
        "x": jax.random.normal(ks[0], (BATCH, SEQ, D), f32),
        "c": jax.random.normal(ks[1], (BATCH, D), f32),
        "w_ada": jax.random.normal(ks[2], (DEPTH, D, 6 * D), f32) * (0.5 * D ** -0.5),
        "b_ada": jax.random.normal(ks[3], (DEPTH, 6 * D), f32) * 0.02,
        "norm1_w": 1.0 + 0.02 * jax.random.normal(ks[4], (DEPTH, D), f32),
        "w_in": jax.random.normal(ks[5], (DEPTH, D, IN_COLS), f32) * D ** -0.5,
        "hgrn_lb_logits": jax.random.normal(ks[6], (DEPTH + 1, HGRN_WIDTH), f32) * 0.5,
        "hgrn_onorm_w": 1.0 + 0.02 * jax.random.normal(ks[7], (DEPTH, HGRN_WIDTH), f32),
        "conv_w": jax.random.normal(ks[8], (DEPTH, CONV_TAPS, CONV_WIDTH), f32) * CONV_TAPS ** -0.5,
        "conv_onorm_w": 1.0 + 0.02 * jax.random.normal(ks[9], (DEPTH, CONV_WIDTH), f32),
        "w_out": jax.random.normal(ks[10], (DEPTH, MIX_WIDTH, D), f32) * MIX_WIDTH ** -0.5,
        "norm2_w": 1.0 + 0.02 * jax.random.normal(ks[11], (DEPTH, D), f32),
        "peer_w_query": jax.random.normal(ks[12], (DEPTH, D, PEER_HEADS * PEER_QDIM), f32) * D ** -0.5,
        "peer_sub_keys": jax.random.normal(ks[13], (DEPTH, PEER_HEADS, 2, PEER_NKEYS, PEER_HALF), f32) * PEER_HALF ** -0.5,
        "peer_u": jax.random.normal(ks[14], (DEPTH, PEER_EXPERTS, D), f32) * D ** -0.5,
        "peer_v": jax.random.normal(ks[15], (DEPTH, PEER_EXPERTS, D), f32),
        "final_norm_w": 1.0 + 0.02 * jax.random.normal(ks[16], (D,), f32),
    }


def reference(x, c, w_ada, b_ada, norm1_w, w_in, hgrn_lb_logits, hgrn_onorm_w, conv_w,
              conv_onorm_w, w_out, norm2_w, peer_w_query, peer_sub_keys, peer_u, peer_v,
              final_norm_w):
    lower_bounds = jnp.cumsum(jax.nn.softmax(hgrn_lb_logits.astype(jnp.float32), axis=0), axis=0)
    c_act = jax.nn.silu(c)
    for l in range(DEPTH):
        mod = (c_act @ w_ada[l] + b_ada[l])[:, None, :]
        shift1, scale1, gate1, shift2, scale2, gate2 = jnp.split(mod, 6, axis=-1)

        h = rmsnorm(x, norm1_w[l]) * (1.0 + scale1) + shift1
        proj = h @ w_in[l]
        y_a = hgrn2_group(proj[..., :4 * HGRN_WIDTH], lower_bounds[l], hgrn_onorm_w[l])
        y_b = short_conv_group(proj[..., 4 * HGRN_WIDTH:], conv_w[l], conv_onorm_w[l])
        mix = jnp.concatenate([y_a, y_b], axis=-1) @ w_out[l]
        x = x + gate1 * mix

        h = rmsnorm(x, norm2_w[l]) * (1.0 + scale2) + shift2
        x = x + gate2 * peer(h, peer_w_query[l], peer_sub_keys[l], peer_u[l], peer_v[l])
    return rmsnorm(x, final_norm_w)
```

```python
import functools
import math

import jax
import jax.numpy as jnp
from jax import lax
from jax.experimental import pallas as pl
from jax.experimental.pallas import tpu as pltpu

F32 = jnp.float32
BF16 = jnp.bfloat16
EPS = 1e-6

LANES = 128
SUBLANES = 8
VMEM_LIMIT_BYTES = 56 * 1024 * 1024

HEAD_DIM = 128
HGRN_HEADS = 4
CONV_GROUPS = 4
PEER_HEADS = 8
PEER_TOPK = 16
LAYER = 0

MOD_TN = 1536
PREP_TE = 512
MIX_TS = 512
CHUNK = 128
DIAG = 8
PEER_TM = 512
PEER_TE = HEAD_DIM * SUBLANES


def _dot(a, b):
    return jnp.dot(a, b, preferred_element_type=F32)


def _dot_nt(a, b):
    return lax.dot_general(a, b, (((1,), (1,)), ((), ())), preferred_element_type=F32)


def _dot_tn(a, b):
    return lax.dot_general(a, b, (((0,), (0,)), ((), ())), preferred_element_type=F32)


def _split3(x):
    hi = x.astype(BF16)
    r1 = x - hi.astype(F32)
    mid = r1.astype(BF16)
    lo = (r1 - mid.astype(F32)).astype(BF16)
    return hi, mid, lo


def _rms(x, w):
    ms = jnp.mean(x * x, axis=-1, keepdims=True)
    return x * lax.rsqrt(ms + EPS) * w


def _mod_kernel(c_ref, w_ref, b_ref, o_ref):
    c = c_ref[...]
    ca = c * jax.nn.sigmoid(c)
    w = w_ref[...]
    ca_hi = ca.astype(BF16)
    ca_lo = (ca - ca_hi.astype(F32)).astype(BF16)
    w_hi = w.astype(BF16)
    w_lo = (w - w_hi.astype(F32)).astype(BF16)
    acc = _dot(ca_hi, w_hi) + _dot(ca_lo, w_hi) + _dot(ca_hi, w_lo)
    o_ref[...] = acc + b_ref[...]


def _modulation(c_pad, w_ada, b_ada):
    rows, d = c_pad.shape
    n = w_ada.shape[1]
    return pl.pallas_call(
        _mod_kernel,
        grid=(n // MOD_TN,),
        in_specs=[
            pl.BlockSpec((rows, d), lambda j: (0, 0)),
            pl.BlockSpec((d, MOD_TN), lambda j: (0, j)),
            pl.BlockSpec((1, MOD_TN), lambda j: (0, j)),
        ],
        out_specs=pl.BlockSpec((rows, MOD_TN), lambda j: (0, j)),
        out_shape=jax.ShapeDtypeStruct((rows, n), F32),
        compiler_params=pltpu.CompilerParams(
            dimension_semantics=("arbitrary",), vmem_limit_bytes=VMEM_LIMIT_BYTES),
        name="adaln_mod",
    )(c_pad, w_ada, b_ada)


def _prep_kernel(u_ref, v_ref, ub_ref, vt_ref):
    ub_ref[...] = u_ref[...].astype(BF16)
    vt_ref[...] = v_ref[...].T.astype(BF16)


def _prep_tables(peer_u, peer_v):
    e, d = peer_u.shape
    return pl.pallas_call(
        _prep_kernel,
        grid=(e // PREP_TE,),
        in_specs=[
            pl.BlockSpec((PREP_TE, d), lambda j: (j, 0)),
            pl.BlockSpec((PREP_TE, d), lambda j: (j, 0)),
        ],
        out_specs=[
            pl.BlockSpec((PREP_TE, d), lambda j: (j, 0)),
            pl.BlockSpec((d, PREP_TE), lambda j: (0, j)),
        ],
        out_shape=[
            jax.ShapeDtypeStruct((e, d), BF16),
            jax.ShapeDtypeStruct((d, e), BF16),
        ],
        compiler_params=pltpu.CompilerParams(
            dimension_semantics=("arbitrary",), vmem_limit_bytes=VMEM_LIMIT_BYTES),
        name="peer_tables",
    )(peer_u, peer_v)


def _mixer_kernel(x_ref, mod_ref, n1w_ref, win_ref, lbl_ref, honw_ref, cw_ref, conw_ref, wout_ref,
                  o_ref, proj_ref, st_ref, ubuf_ref, ycat_ref, kk_ref, bb_ref, sc_ref):
    ts, d = x_ref.shape[1], x_ref.shape[2]
    hw = HGRN_HEADS * HEAD_DIM
    cwid = CONV_GROUPS * HEAD_DIM
    n_chunks = ts // CHUNK

    @pl.when(pl.program_id(1) == 0)
    def _():
        st_ref[...] = jnp.zeros_like(st_ref)
        ubuf_ref[0:SUBLANES, :] = jnp.zeros((SUBLANES, cwid), F32)

    x = x_ref[0]
    mod = mod_ref[0]
    shift1 = mod[:, 0:d]
    scale1 = mod[:, d:2 * d]
    gate1 = mod[:, 2 * d:3 * d]
    h = _rms(x, n1w_ref[...]) * (1.0 + scale1) + shift1
    proj_ref[...] = _dot(h.astype(BF16), win_ref[...])

    lbl = lbl_ref[...]
    lbe = jnp.exp(lbl - jnp.max(lbl, axis=0, keepdims=True))
    lbp = lbe / jnp.sum(lbe, axis=0, keepdims=True)
    lb = jnp.sum(lbp[0:LAYER + 1, :], axis=0, keepdims=True)

    row = lax.broadcasted_iota(jnp.int32, (CHUNK, CHUNK), 0)
    col = lax.broadcasted_iota(jnp.int32, (CHUNK, CHUNK), 1)
    tril = jnp.where(col <= row, 1.0, 0.0).astype(BF16)
    rowd = lax.broadcasted_iota(jnp.int32, (CHUNK, HEAD_DIM), 0)
    sub8 = lax.broadcasted_iota(jnp.int32, (DIAG, HEAD_DIM), 0)
    lane8 = lax.broadcasted_iota(jnp.int32, (DIAG, CHUNK), 1)
    halves = []
    hs = CHUNK // 2
    while hs >= DIAG:
        halves.append(hs)
        hs //= 2

    def chunk_body(c, carry):
        r0 = pl.multiple_of(c * CHUNK, CHUNK)
        fraw = proj_ref[pl.ds(r0, CHUNK), hw:2 * hw]
        f = lb + (1.0 - lb) * jax.nn.sigmoid(fraw)
        logf = jnp.log(f)
        kk = (1.0 - lb) * jax.nn.sigmoid(-fraw)
        hi, mid, lo = _split3(logf)
        b_all = _dot(tril, hi) + _dot(tril, mid) + _dot(tril, lo)
        kk_ref[...] = kk
        bb_ref[...] = b_all

        def diag_body(j, carry2):
            rr = pl.multiple_of(j * DIAG, DIAG)
            for hd in range(HGRN_HEADS):
                sl = slice(hd * HEAD_DIM, (hd + 1) * HEAD_DIM)
                qb = proj_ref[pl.ds(r0 + rr, DIAG), sl]
                kb = kk_ref[pl.ds(rr, DIAG), sl]
                bb = bb_ref[pl.ds(rr, DIAG), sl]
                slab = jnp.zeros((DIAG, CHUNK), F32)
                for s in range(DIAG):
                    arg = jnp.where(sub8 >= s, bb - bb[s:s + 1, :], -1e30)
                    p = qb * kb[s:s + 1, :] * jnp.exp(arg)
                    colsum = jnp.sum(p, axis=-1, keepdims=True)
                    slab = jnp.where(lane8 == rr + s, colsum, slab)
                sc_ref[hd, pl.ds(rr, DIAG), :] = slab
            return carry2

        lax.fori_loop(0, CHUNK // DIAG, diag_body, 0)

        for hd in range(HGRN_HEADS):
            sl = slice(hd * HEAD_DIM, (hd + 1) * HEAD_DIM)
            q = proj_ref[pl.ds(r0, CHUNK), sl]
            v = proj_ref[pl.ds(r0, CHUNK), 2 * hw + hd * HEAD_DIM:2 * hw + (hd + 1) * HEAD_DIM]
            gout = proj_ref[pl.ds(r0, CHUNK), 3 * hw + hd * HEAD_DIM:3 * hw + (hd + 1) * HEAD_DIM]
            k = kk[:, sl]
            b = b_all[:, sl]
            scores = sc_ref[hd]
            for hs in halves:
                parts = []
                for blk in range(CHUNK // (2 * hs)):
                    r = blk * 2 * hs + hs - 1
                    parts.append(jnp.broadcast_to(b[r:r + 1, :], (2 * hs, HEAD_DIM)))
                bref = parts[0] if len(parts) == 1 else jnp.concatenate(parts, axis=0)
                dd = b - bref
                second = ((rowd // hs) % 2) == 1
                e = jnp.exp(jnp.where(second, dd, -dd))
                ql = jnp.where(second, q * e, 0.0)
                kl = jnp.where(second, 0.0, k * e)
                sl_scores = _dot_nt(ql.astype(BF16), kl.astype(BF16))
                valid = ((row // (2 * hs)) == (col // (2 * hs))) & (((row // hs) % 2) == 1) & (((col // hs) % 2) == 0)
                scores = jnp.where(valid, sl_scores, scores)
            st = st_ref[hd]
            o = _dot(scores.astype(BF16), v.astype(BF16))
            o = o + _dot_nt((q * jnp.exp(b)).astype(BF16), st.astype(BF16))
            b_end = b[CHUNK - 1:CHUNK, :]
            ks = k * jnp.exp(b_end - b)
            st_ref[hd] = st * jnp.exp(b_end) + _dot_tn(v.astype(BF16), ks.astype(BF16))
            oh = _rms(o, honw_ref[:, sl]) * (gout * jax.nn.sigmoid(gout))
            ycat_ref[pl.ds(r0, CHUNK), sl] = oh
        return carry

    lax.fori_loop(0, n_chunks, chunk_body, 0)

    bg = proj_ref[:, 4 * hw:4 * hw + cwid]
    cg = proj_ref[:, 4 * hw + cwid:4 * hw + 2 * cwid]
    xin = proj_ref[:, 4 * hw + 2 * cwid:4 * hw + 3 * cwid]
    u = cg * xin
    ubuf_ref[SUBLANES:SUBLANES + ts, :] = u
    cw = cw_ref[...]
    y = (cw[0:1, :] * ubuf_ref[SUBLANES - 2:SUBLANES - 2 + ts, :]
         + cw[1:2, :] * ubuf_ref[SUBLANES - 1:SUBLANES - 1 + ts, :]
         + cw[2:3, :] * u)
    ubuf_ref[0:SUBLANES, :] = u[ts - SUBLANES:ts, :]
    y = bg * y
    for g in range(CONV_GROUPS):
        sl = slice(g * HEAD_DIM, (g + 1) * HEAD_DIM)
        ycat_ref[:, hw + g * HEAD_DIM:hw + (g + 1) * HEAD_DIM] = _rms(y[:, sl], conw_ref[:, sl])

    mix = _dot(ycat_ref[...].astype(BF16), wout_ref[...])
    o_ref[0] = x + gate1 * mix


def _mixer(x, mod3, norm1_w, w_in_b, lb_logits, honw, conv_w, conw, w_out_b):
    bsz, seq, d = x.shape
    ncols = w_in_b.shape[1]
    hw = HGRN_HEADS * HEAD_DIM
    cwid = CONV_GROUPS * HEAD_DIM
    const2 = lambda b, s: (0, 0)
    return pl.pallas_call(
        _mixer_kernel,
        grid=(bsz, seq // MIX_TS),
        in_specs=[
            pl.BlockSpec((1, MIX_TS, d), lambda b, s: (b, s, 0)),
            pl.BlockSpec((1, 1, mod3.shape[2]), lambda b, s: (b, 0, 0)),
            pl.BlockSpec((1, d), const2),
            pl.BlockSpec((d, ncols), const2, pipeline_mode=pl.Buffered(1)),
            pl.BlockSpec(lb_logits.shape, const2),
            pl.BlockSpec((1, hw), const2),
            pl.BlockSpec(conv_w.shape, const2),
            pl.BlockSpec((1, cwid), const2),
            pl.BlockSpec((hw + cwid, d), const2, pipeline_mode=pl.Buffered(1)),
        ],
        out_specs=pl.BlockSpec((1, MIX_TS, d), lambda b, s: (b, s, 0)),
        out_shape=jax.ShapeDtypeStruct((bsz, seq, d), F32),
        scratch_shapes=[
            pltpu.VMEM((MIX_TS, ncols), F32),
            pltpu.VMEM((HGRN_HEADS, HEAD_DIM, HEAD_DIM), F32),
            pltpu.VMEM((SUBLANES + MIX_TS, cwid), F32),
            pltpu.VMEM((MIX_TS, hw + cwid), F32),
            pltpu.VMEM((CHUNK, hw), F32),
            pltpu.VMEM((CHUNK, hw), F32),
            pltpu.VMEM((HGRN_HEADS, CHUNK, CHUNK), F32),
        ],
        compiler_params=pltpu.CompilerParams(
            dimension_semantics=("arbitrary", "arbitrary"), vmem_limit_bytes=VMEM_LIMIT_BYTES),
        name="token_mixer",
    )(x, mod3, norm1_w, w_in_b, lb_logits, honw, conv_w, conw, w_out_b)


def _candidates():
    return [(i, j) for i in range(PEER_TOPK) for j in range(PEER_TOPK)
            if (i + 1) * (j + 1) <= PEER_TOPK]


def _peer_kernel(x_ref, mod_ref, n2w_ref, wq_ref, keys_ref, u_ref, vt_ref, fw_ref,
                 o_ref, h2t_ref, qt_ref, acc_ref, s_ref, rank_ref, vals_ref, act_ref, p_ref):
    tm, d = x_ref.shape
    te = u_ref.shape[0]
    j = pl.program_id(1)
    n_tb = tm // LANES
    n_sub = HEAD_DIM
    mod = mod_ref[0]

    @pl.when(j == 0)
    def _route():
        shift2 = mod[:, 3 * d:4 * d]
        scale2 = mod[:, 4 * d:5 * d]
        h2 = _rms(x_ref[...], n2w_ref[...]) * (1.0 + scale2) + shift2
        h2b = h2.astype(BF16)
        h2t_ref[...] = h2.T.astype(BF16)
        qt_ref[...] = _dot(h2b, wq_ref[...]).T.astype(BF16)
        acc_ref[...] = jnp.zeros_like(acc_ref)

        n_iota = lax.broadcasted_iota(jnp.int32, (n_sub, LANES), 0).astype(F32)
        head_iota = lax.broadcasted_iota(jnp.int32, (PEER_HEADS, LANES), 0)
        cands = _candidates()

        def tb_body(tb, carry):
            lanes = pl.ds(pl.multiple_of(tb * LANES, LANES), LANES)

            def hp_body(hp, carry2):
                qrows = pl.ds(pl.multiple_of(hp * HEAD_DIM, HEAD_DIM), HEAD_DIM)
                s = _dot(keys_ref[hp], qt_ref[qrows, lanes])
                s_ref[hp, :, lanes] = s
                rank = jnp.full((n_sub, LANES), float(PEER_TOPK), F32)
                hh = hp // 2
                pp = hp % 2
                for i in range(PEER_TOPK):
                    m = jnp.max(s, axis=0, keepdims=True)
                    first = jnp.min(jnp.where(s == m, n_iota, float(n_sub)), axis=0, keepdims=True)
                    hit = n_iota == first
                    rank = jnp.where(hit, float(i), rank)
                    s = jnp.where(hit, -jnp.inf, s)
                    vals_ref[pp, i, :, lanes] = jnp.where(head_iota == hh, m, vals_ref[pp, i, :, lanes])
                rank_ref[hp, :, lanes] = rank
                return carry2

            lax.fori_loop(0, 2 * PEER_HEADS, hp_body, 0)

            v1 = [vals_ref[0, i, :, lanes] for i in range(PEER_TOPK)]
            v2 = [vals_ref[1, i, :, lanes] for i in range(PEER_TOPK)]
            c = [v1[i] + v2[jj] for (i, jj) in cands]
            c00 = c[0]
            cnt = [jnp.zeros((PEER_HEADS, LANES), F32) for _ in range(PEER_TOPK)]
            z = jnp.zeros((PEER_HEADS, LANES), F32)
            big = float(PEER_TOPK * PEER_TOPK)
            for _ in range(PEER_TOPK):
                m = functools.reduce(jnp.maximum, c)
                first = functools.reduce(
                    jnp.minimum,
                    [jnp.where(c[k] == m, float(i * PEER_TOPK + jj), big) for k, (i, jj) in enumerate(cands)])
                z = z + jnp.exp(m - c00)
                for k, (i, jj) in enumerate(cands):
                    hit = first == float(i * PEER_TOPK + jj)
                    cnt[i] = cnt[i] + jnp.where(hit, 1.0, 0.0)
                    c[k] = jnp.where(hit, -jnp.inf, c[k])
            zinv = 1.0 / z

            for hh in range(PEER_HEADS):
                rank1 = rank_ref[2 * hh, :, lanes]
                n1 = jnp.zeros((n_sub, LANES), F32)
                for i in range(PEER_TOPK):
                    n1 = jnp.where(rank1 == float(i), cnt[i][hh:hh + 1, :], n1)
                rank_ref[2 * hh, :, lanes] = n1
                s1 = s_ref[2 * hh, :, lanes]
                s2 = s_ref[2 * hh + 1, :, lanes]
                s_ref[2 * hh, :, lanes] = jnp.exp(s1 - v1[0][hh:hh + 1, :])
                s_ref[2 * hh + 1, :, lanes] = jnp.exp(s2 - v2[0][hh:hh + 1, :]) * zinv[hh:hh + 1, :]
            return carry

        lax.fori_loop(0, n_tb, tb_body, 0)

    act_ref[...] = _dot(u_ref[...], h2t_ref[...])
    n_a = te // n_sub
    assert n_a == SUBLANES
    arows = pl.ds(pl.multiple_of(j * n_a, SUBLANES), SUBLANES)

    def w_body(tb, carry):
        lanes = pl.ds(pl.multiple_of(tb * LANES, LANES), LANES)
        for al in range(n_a):
            w = jnp.zeros((n_sub, LANES), F32)
            for hh in range(PEER_HEADS):
                n1 = rank_ref[2 * hh, arows, lanes][al:al + 1, :]
                e1 = s_ref[2 * hh, arows, lanes][al:al + 1, :]
                r2 = rank_ref[2 * hh + 1, :, lanes]
                e2 = s_ref[2 * hh + 1, :, lanes]
                w = w + jnp.where(r2 < n1, e2, 0.0) * e1
            xa = act_ref[al * n_sub:(al + 1) * n_sub, lanes]
            g = 0.5 * xa * (1.0 + lax.erf(xa * math.sqrt(0.5)))
            p_ref[al * n_sub:(al + 1) * n_sub, lanes] = (g * w).astype(BF16)
        return carry

    lax.fori_loop(0, n_tb, w_body, 0)
    acc_ref[...] += _dot(vt_ref[...], p_ref[...])

    @pl.when(j == pl.num_programs(1) - 1)
    def _finish():
        gate2 = mod[:, 5 * d:6 * d]
        x2 = x_ref[...] + gate2 * acc_ref[...].T
        o_ref[...] = _rms(x2, fw_ref[...])


def _peer(x1, mod3, seq, norm2_w, wq_b, keys_b, u_b, vt_b, final_w):
    t, d = x1.shape
    e = u_b.shape[0]
    qcols = wq_b.shape[1]
    tiles_per_seq = seq // PEER_TM
    const2 = lambda i, j: (0, 0)
    return pl.pallas_call(
        _peer_kernel,
        grid=(t // PEER_TM, e // PEER_TE),
        in_specs=[
            pl.BlockSpec((PEER_TM, d), lambda i, j: (i, 0)),
            pl.BlockSpec((1, 1, mod3.shape[2]), lambda i, j: (i // tiles_per_seq, 0, 0)),
            pl.BlockSpec((1, d), const2),
            pl.BlockSpec((d, qcols), const2, pipeline_mode=pl.Buffered(1)),
            pl.BlockSpec(keys_b.shape, lambda i, j: (0, 0, 0), pipeline_mode=pl.Buffered(1)),
            pl.BlockSpec((PEER_TE, d), lambda i, j: (j, 0)),
            pl.BlockSpec((d, PEER_TE), lambda i, j: (0, j)),
            pl.BlockSpec((1, d), const2),
        ],
        out_specs=pl.BlockSpec((PEER_TM, d), lambda i, j: (i, 0)),
        out_shape=jax.ShapeDtypeStruct((t, d), F32),
        scratch_shapes=[
            pltpu.VMEM((d, PEER_TM), BF16),
            pltpu.VMEM((qcols, PEER_TM), BF16),
            pltpu.VMEM((d, PEER_TM), F32),
            pltpu.VMEM((2 * PEER_HEADS, HEAD_DIM, PEER_TM), F32),
            pltpu.VMEM((2 * PEER_HEADS, HEAD_DIM, PEER_TM), F32),
            pltpu.VMEM((2, PEER_TOPK, PEER_HEADS, PEER_TM), F32),
            pltpu.VMEM((PEER_TE, PEER_TM), F32),
            pltpu.VMEM((PEER_TE, PEER_TM), BF16),
        ],
        compiler_params=pltpu.CompilerParams(
            dimension_semantics=("arbitrary", "arbitrary"), vmem_limit_bytes=VMEM_LIMIT_BYTES),
        name="peer_dense",
    )(x1, mod3, norm2_w, wq_b, keys_b, u_b, vt_b, final_w)


def kernel(x, c, w_ada, b_ada, norm1_w, w_in, hgrn_lb_logits, hgrn_onorm_w, conv_w, conv_onorm_w,
           w_out, norm2_w, peer_w_query, peer_sub_keys, peer_u, peer_v, final_norm_w):
    bsz, seq, d = x.shape
    assert w_ada.shape[0] == 1, "single-layer kernel"
    assert seq % MIX_TS == 0 and seq % PEER_TM == 0 and MIX_TS % CHUNK == 0
    assert peer_u.shape[1] % PEER_TE == 0 and PEER_TE % HEAD_DIM == 0

    c_pad = jnp.pad(c, ((0, SUBLANES - bsz), (0, 0)))
    mod = _modulation(c_pad, w_ada[0], b_ada[0][None, :])
    mod3 = mod[:bsz].reshape(bsz, 1, mod.shape[1])

    u_b, vt_b = _prep_tables(peer_u[0], peer_v[0])

    x1 = _mixer(x, mod3, norm1_w[0][None, :], w_in[0].astype(BF16), hgrn_lb_logits,
                hgrn_onorm_w[0][None, :], conv_w[0], conv_onorm_w[0][None, :], w_out[0].astype(BF16))

    keys_b = peer_sub_keys[0].reshape(2 * PEER_HEADS, HEAD_DIM, HEAD_DIM).astype(BF16)
    out = _peer(x1.reshape(bsz * seq, d), mod3, seq, norm2_w[0][None, :],
                peer_w_query[0].astype(BF16), keys_b, u_b, vt_b, final_norm_w[None, :])
    return out.reshape(bsz, seq, d)
```

```python
import functools
import math

import jax
import jax.numpy as jnp
from jax import lax
from jax.experimental import pallas as pl
from jax.experimental.pallas import tpu as pltpu

F32 = jnp.float32
BF16 = jnp.bfloat16
EPS = 1e-6

LANES = 128
SUBLANES = 8
VMEM_LIMIT_BYTES = 56 * 1024 * 1024

HEAD_DIM = 128
HGRN_HEADS = 4
CONV_GROUPS = 4
PEER_HEADS = 8
PEER_TOPK = 16
LAYER = 0

MOD_TN = 1536
PREP_TE = 512
MIX_TS = 512
CHUNK = 128
DIAG = 8
PEER_TM = 512
PEER_TE = HEAD_DIM * SUBLANES
PEER_CB = 256


def _dot(a, b):
    return jnp.dot(a, b, preferred_element_type=F32)


def _dot_nt(a, b):
    return lax.dot_general(a, b, (((1,), (1,)), ((), ())), preferred_element_type=F32)


def _dot_tn(a, b):
    return lax.dot_general(a, b, (((0,), (0,)), ((), ())), preferred_element_type=F32)


def _split3(x):
    hi = x.astype(BF16)
    r1 = x - hi.astype(F32)
    mid = r1.astype(BF16)
    lo = (r1 - mid.astype(F32)).astype(BF16)
    return hi, mid, lo


def _rms(x, w):
    ms = jnp.mean(x * x, axis=-1, keepdims=True)
    return x * lax.rsqrt(ms + EPS) * w


def _mod_kernel(c_ref, w_ref, b_ref, o_ref):
    c = c_ref[...]
    ca = c * jax.nn.sigmoid(c)
    w = w_ref[...]
    ca_hi = ca.astype(BF16)
    ca_lo = (ca - ca_hi.astype(F32)).astype(BF16)
    w_hi = w.astype(BF16)
    w_lo = (w - w_hi.astype(F32)).astype(BF16)
    acc = _dot(ca_hi, w_hi) + _dot(ca_lo, w_hi) + _dot(ca_hi, w_lo)
    o_ref[...] = acc + b_ref[...]


def _modulation(c_pad, w_ada, b_ada):
    rows, d = c_pad.shape
    n = w_ada.shape[1]
    return pl.pallas_call(
        _mod_kernel,
        grid=(n // MOD_TN,),
        in_specs=[
            pl.BlockSpec((rows, d), lambda j: (0, 0)),
            pl.BlockSpec((d, MOD_TN), lambda j: (0, j)),
            pl.BlockSpec((1, MOD_TN), lambda j: (0, j)),
        ],
        out_specs=pl.BlockSpec((rows, MOD_TN), lambda j: (0, j)),
        out_shape=jax.ShapeDtypeStruct((rows, n), F32),
        compiler_params=pltpu.CompilerParams(
            dimension_semantics=("arbitrary",), vmem_limit_bytes=VMEM_LIMIT_BYTES),
        name="adaln_mod",
    )(c_pad, w_ada, b_ada)


def _prep_kernel(u_ref, v_ref, ub_ref, vt_ref):
    ub_ref[...] = u_ref[...].astype(BF16)
    vt_ref[...] = v_ref[...].T.astype(BF16)


def _prep_tables(peer_u, peer_v):
    e, d = peer_u.shape
    return pl.pallas_call(
        _prep_kernel,
        grid=(e // PREP_TE,),
        in_specs=[
            pl.BlockSpec((PREP_TE, d), lambda j: (j, 0)),
            pl.BlockSpec((PREP_TE, d), lambda j: (j, 0)),
        ],
        out_specs=[
            pl.BlockSpec((PREP_TE, d), lambda j: (j, 0)),
            pl.BlockSpec((d, PREP_TE), lambda j: (0, j)),
        ],
        out_shape=[
            jax.ShapeDtypeStruct((e, d), BF16),
            jax.ShapeDtypeStruct((d, e), BF16),
        ],
        compiler_params=pltpu.CompilerParams(
            dimension_semantics=("arbitrary",), vmem_limit_bytes=VMEM_LIMIT_BYTES),
        name="peer_tables",
    )(peer_u, peer_v)


def _mixer_kernel(x_ref, mod_ref, n1w_ref, win_ref, lbl_ref, honw_ref, cw_ref, conw_ref, wout_ref,
                  o_ref, proj_ref, st_ref, ubuf_ref, ycat_ref, kk_ref, bb_ref, sc_ref):
    ts, d = x_ref.shape[1], x_ref.shape[2]
    hw = HGRN_HEADS * HEAD_DIM
    cwid = CONV_GROUPS * HEAD_DIM
    n_chunks = ts // CHUNK

    @pl.when(pl.program_id(1) == 0)
    def _():
        st_ref[...] = jnp.zeros_like(st_ref)
        ubuf_ref[0:SUBLANES, :] = jnp.zeros((SUBLANES, cwid), F32)

    x = x_ref[0]
    mod = mod_ref[0]
    shift1 = mod[:, 0:d]
    scale1 = mod[:, d:2 * d]
    gate1 = mod[:, 2 * d:3 * d]
    h = _rms(x, n1w_ref[...]) * (1.0 + scale1) + shift1
    proj_ref[...] = _dot(h.astype(BF16), win_ref[...])

    lbl = lbl_ref[...]
    lbe = jnp.exp(lbl - jnp.max(lbl, axis=0, keepdims=True))
    lbp = lbe / jnp.sum(lbe, axis=0, keepdims=True)
    lb = jnp.sum(lbp[0:LAYER + 1, :], axis=0, keepdims=True)

    row = lax.broadcasted_iota(jnp.int32, (CHUNK, CHUNK), 0)
    col = lax.broadcasted_iota(jnp.int32, (CHUNK, CHUNK), 1)
    tril = jnp.where(col <= row, 1.0, 0.0).astype(BF16)
    rowd = lax.broadcasted_iota(jnp.int32, (CHUNK, HEAD_DIM), 0)
    sub8 = lax.broadcasted_iota(jnp.int32, (DIAG, HEAD_DIM), 0)
    lane8 = lax.broadcasted_iota(jnp.int32, (DIAG, CHUNK), 1)
    halves = []
    hs = CHUNK // 2
    while hs >= DIAG:
        halves.append(hs)
        hs //= 2

    def chunk_body(c, carry):
        r0 = pl.multiple_of(c * CHUNK, CHUNK)
        fraw = proj_ref[pl.ds(r0, CHUNK), hw:2 * hw]
        f = lb + (1.0 - lb) * jax.nn.sigmoid(fraw)
        logf = jnp.log(f)
        kk = (1.0 - lb) * jax.nn.sigmoid(-fraw)
        hi, mid, lo = _split3(logf)
        b_all = _dot(tril, hi) + _dot(tril, mid) + _dot(tril, lo)
        kk_ref[...] = kk
        bb_ref[...] = b_all

        def diag_body(j, carry2):
            rr = pl.multiple_of(j * DIAG, DIAG)
            for hd in range(HGRN_HEADS):
                sl = slice(hd * HEAD_DIM, (hd + 1) * HEAD_DIM)
                qb = proj_ref[pl.ds(r0 + rr, DIAG), sl]
                kb = kk_ref[pl.ds(rr, DIAG), sl]
                bb = bb_ref[pl.ds(rr, DIAG), sl]
                slab = jnp.zeros((DIAG, CHUNK), F32)
                for s in range(DIAG):
                    arg = jnp.where(sub8 >= s, bb - bb[s:s + 1, :], -1e30)
                    p = qb * kb[s:s + 1, :] * jnp.exp(arg)
                    colsum = jnp.sum(p, axis=-1, keepdims=True)
                    slab = jnp.where(lane8 == rr + s, colsum, slab)
                sc_ref[hd, pl.ds(rr, DIAG), :] = slab
            return carry2

        lax.fori_loop(0, CHUNK // DIAG, diag_body, 0)

        for hd in range(HGRN_HEADS):
            sl = slice(hd * HEAD_DIM, (hd + 1) * HEAD_DIM)
            q = proj_ref[pl.ds(r0, CHUNK), sl]
            v = proj_ref[pl.ds(r0, CHUNK), 2 * hw + hd * HEAD_DIM:2 * hw + (hd + 1) * HEAD_DIM]
            gout = proj_ref[pl.ds(r0, CHUNK), 3 * hw + hd * HEAD_DIM:3 * hw + (hd + 1) * HEAD_DIM]
            k = kk[:, sl]
            b = b_all[:, sl]
            scores = sc_ref[hd]
            for hs in halves:
                parts = []
                for blk in range(CHUNK // (2 * hs)):
                    r = blk * 2 * hs + hs - 1
                    parts.append(jnp.broadcast_to(b[r:r + 1, :], (2 * hs, HEAD_DIM)))
                bref = parts[0] if len(parts) == 1 else jnp.concatenate(parts, axis=0)
                dd = b - bref
                second = ((rowd // hs) % 2) == 1
                e = jnp.exp(jnp.where(second, dd, -dd))
                ql = jnp.where(second, q * e, 0.0)
                kl = jnp.where(second, 0.0, k * e)
                sl_scores = _dot_nt(ql.astype(BF16), kl.astype(BF16))
                valid = ((row // (2 * hs)) == (col // (2 * hs))) & (((row // hs) % 2) == 1) & (((col // hs) % 2) == 0)
                scores = jnp.where(valid, sl_scores, scores)
            st = st_ref[hd]
            o = _dot(scores.astype(BF16), v.astype(BF16))
            o = o + _dot_nt((q * jnp.exp(b)).astype(BF16), st.astype(BF16))
            b_end = b[CHUNK - 1:CHUNK, :]
            ks = k * jnp.exp(b_end - b)
            st_ref[hd] = st * jnp.exp(b_end) + _dot_tn(v.astype(BF16), ks.astype(BF16))
            oh = _rms(o, honw_ref[:, sl]) * (gout * jax.nn.sigmoid(gout))
            ycat_ref[pl.ds(r0, CHUNK), sl] = oh
        return carry

    lax.fori_loop(0, n_chunks, chunk_body, 0)

    bg = proj_ref[:, 4 * hw:4 * hw + cwid]
    cg = proj_ref[:, 4 * hw + cwid:4 * hw + 2 * cwid]
    xin = proj_ref[:, 4 * hw + 2 * cwid:4 * hw + 3 * cwid]
    u = cg * xin
    ubuf_ref[SUBLANES:SUBLANES + ts, :] = u
    cw = cw_ref[...]
    y = (cw[0:1, :] * ubuf_ref[SUBLANES - 2:SUBLANES - 2 + ts, :]
         + cw[1:2, :] * ubuf_ref[SUBLANES - 1:SUBLANES - 1 + ts, :]
         + cw[2:3, :] * u)
    ubuf_ref[0:SUBLANES, :] = u[ts - SUBLANES:ts, :]
    y = bg * y
    for g in range(CONV_GROUPS):
        sl = slice(g * HEAD_DIM, (g + 1) * HEAD_DIM)
        ycat_ref[:, hw + g * HEAD_DIM:hw + (g + 1) * HEAD_DIM] = _rms(y[:, sl], conw_ref[:, sl])

    mix = _dot(ycat_ref[...].astype(BF16), wout_ref[...])
    o_ref[0] = x + gate1 * mix


def _mixer(x, mod3, norm1_w, w_in_b, lb_logits, honw, conv_w, conw, w_out_b):
    bsz, seq, d = x.shape
    ncols = w_in_b.shape[1]
    hw = HGRN_HEADS * HEAD_DIM
    cwid = CONV_GROUPS * HEAD_DIM
    const2 = lambda b, s: (0, 0)
    return pl.pallas_call(
        _mixer_kernel,
        grid=(bsz, seq // MIX_TS),
        in_specs=[
            pl.BlockSpec((1, MIX_TS, d), lambda b, s: (b, s, 0)),
            pl.BlockSpec((1, 1, mod3.shape[2]), lambda b, s: (b, 0, 0)),
            pl.BlockSpec((1, d), const2),
            pl.BlockSpec((d, ncols), const2, pipeline_mode=pl.Buffered(1)),
            pl.BlockSpec(lb_logits.shape, const2),
            pl.BlockSpec((1, hw), const2),
            pl.BlockSpec(conv_w.shape, const2),
            pl.BlockSpec((1, cwid), const2),
            pl.BlockSpec((hw + cwid, d), const2, pipeline_mode=pl.Buffered(1)),
        ],
        out_specs=pl.BlockSpec((1, MIX_TS, d), lambda b, s: (b, s, 0)),
        out_shape=jax.ShapeDtypeStruct((bsz, seq, d), F32),
        scratch_shapes=[
            pltpu.VMEM((MIX_TS, ncols), F32),
            pltpu.VMEM((HGRN_HEADS, HEAD_DIM, HEAD_DIM), F32),
            pltpu.VMEM((SUBLANES + MIX_TS, cwid), F32),
            pltpu.VMEM((MIX_TS, hw + cwid), F32),
            pltpu.VMEM((CHUNK, hw), F32),
            pltpu.VMEM((CHUNK, hw), F32),
            pltpu.VMEM((HGRN_HEADS, CHUNK, CHUNK), F32),
        ],
        compiler_params=pltpu.CompilerParams(
            dimension_semantics=("arbitrary", "arbitrary"), vmem_limit_bytes=VMEM_LIMIT_BYTES),
        name="token_mixer",
    )(x, mod3, norm1_w, w_in_b, lb_logits, honw, conv_w, conw, w_out_b)


def _candidates():
    return [(i, j) for i in range(PEER_TOPK) for j in range(PEER_TOPK)
            if (i + 1) * (j + 1) <= PEER_TOPK]


def _peer_kernel(x_ref, mod_ref, n2w_ref, wq_ref, keys_ref, u_ref, vt_ref, fw_ref,
                 o_ref, h2t_ref, qt_ref, acc_ref, st_ref, rk_ref, vals_ref,
                 n1f_ref, e1f_ref, r2b_ref, e2b_ref, bcn_ref, bce_ref, act_ref, p_ref):
    tm, d = x_ref.shape
    te = u_ref.shape[0]
    j = pl.program_id(1)
    n_tb = tm // LANES
    n_sub = HEAD_DIM
    mod = mod_ref[0]

    @pl.when(j == 0)
    def _route():
        shift2 = mod[:, 3 * d:4 * d]
        scale2 = mod[:, 4 * d:5 * d]
        h2 = _rms(x_ref[...], n2w_ref[...]) * (1.0 + scale2) + shift2
        h2b = h2.astype(BF16)
        h2t_ref[...] = h2.T.astype(BF16)
        qt_ref[...] = _dot(h2b, wq_ref[...]).T.astype(BF16)
        acc_ref[...] = jnp.zeros_like(acc_ref)

        n_iota = lax.broadcasted_iota(jnp.int32, (n_sub, LANES), 0).astype(F32)
        head_iota = lax.broadcasted_iota(jnp.int32, (PEER_HEADS, LANES), 0)
        cands = _candidates()

        def tb_body(tb, carry):
            lanes = pl.ds(pl.multiple_of(tb * LANES, LANES), LANES)

            def extract(hp, hh, pp):
                qrows = pl.ds(pl.multiple_of(hp * HEAD_DIM, HEAD_DIM), HEAD_DIM)
                s = _dot(keys_ref[hp], qt_ref[qrows, lanes])
                st_ref[hp] = s
                rk_ref[hp] = jnp.full((n_sub, LANES), float(PEER_TOPK), F32)
                for i in range(PEER_TOPK):
                    m = jnp.max(s, axis=0, keepdims=True)
                    first = jnp.min(jnp.where(s == m, n_iota, float(n_sub)), axis=0, keepdims=True)
                    hit = n_iota == first
                    rk_ref[hp] = jnp.where(hit, float(i), rk_ref[hp])
                    s = jnp.where(hit, -jnp.inf, s)
                    vals_ref[pp, i] = jnp.where(head_iota == hh, m, vals_ref[pp, i])

            def hh_body(h2i, carry2):
                for dh in range(2):
                    hh = 2 * h2i + dh
                    for pp in range(2):
                        extract(2 * hh + pp, hh, pp)
                return carry2

            lax.fori_loop(0, PEER_HEADS // 2, hh_body, 0)

            v1 = [vals_ref[0, i] for i in range(PEER_TOPK)]
            v2 = [vals_ref[1, i] for i in range(PEER_TOPK)]
            c = [v1[i] + v2[jj] for (i, jj) in cands]
            c00 = c[0]
            cnt = [jnp.zeros((PEER_HEADS, LANES), F32) for _ in range(PEER_TOPK)]
            z = jnp.zeros((PEER_HEADS, LANES), F32)
            big = float(PEER_TOPK * PEER_TOPK)
            for _ in range(PEER_TOPK):
                m = functools.reduce(jnp.maximum, c)
                first = functools.reduce(
                    jnp.minimum,
                    [jnp.where(c[k] == m, float(i * PEER_TOPK + jj), big) for k, (i, jj) in enumerate(cands)])
                z = z + jnp.exp(m - c00)
                for k, (i, jj) in enumerate(cands):
                    hit = first == float(i * PEER_TOPK + jj)
                    cnt[i] = cnt[i] + jnp.where(hit, 1.0, 0.0)
                    c[k] = jnp.where(hit, -jnp.inf, c[k])
            zinv = 1.0 / z

            for hh in range(PEER_HEADS):
                rank1 = rk_ref[2 * hh]
                n1 = jnp.zeros((n_sub, LANES), F32)
                for i in range(PEER_TOPK):
                    n1 = jnp.where(rank1 == float(i), cnt[i][hh:hh + 1, :], n1)
                n1f_ref[hh, tb] = n1
                e1f_ref[hh, tb] = jnp.exp(st_ref[2 * hh] - v1[0][hh:hh + 1, :])
                r2b_ref[hh, tb] = rk_ref[2 * hh + 1].astype(BF16)
                e2 = jnp.exp(st_ref[2 * hh + 1] - v2[0][hh:hh + 1, :]) * zinv[hh:hh + 1, :]
                e2b_ref[hh, tb] = e2.astype(BF16)
            return carry

        lax.fori_loop(0, n_tb, tb_body, 0)

    n_a = te // n_sub
    assert n_a == SUBLANES
    arows = pl.ds(pl.multiple_of(j * n_a, SUBLANES), SUBLANES)
    pack = 2 * SUBLANES
    zero_b = jnp.zeros((pack, LANES), BF16)
    n_cb = tm // PEER_CB
    tb_per_cb = PEER_CB // LANES
    nb_group = 2

    for cb in range(n_cb):
        cl = slice(cb * PEER_CB, (cb + 1) * PEER_CB)
        act_ref[:, cl] = _dot(u_ref[...], h2t_ref[:, cl])

    for cb in range(n_cb):
        cl = slice(cb * PEER_CB, (cb + 1) * PEER_CB)
        for tb in range(cb * tb_per_cb, (cb + 1) * tb_per_cb):
            lanes = slice(tb * LANES, (tb + 1) * LANES)
            for hh in range(PEER_HEADS):
                n1t = n1f_ref[hh, tb, arows, :]
                e1t = e1f_ref[hh, tb, arows, :]
                for al in range(n_a):
                    bcn_ref[tb, hh, al] = jnp.broadcast_to(n1t[al:al + 1, :], (pack, LANES)).astype(BF16)
                    bce_ref[tb, hh, al] = jnp.broadcast_to(e1t[al:al + 1, :], (pack, LANES)).astype(BF16)
            for nq in range(n_sub // (nb_group * pack)):
                w = [[zero_b for _ in range(n_a)] for _ in range(nb_group)]
                for hh in range(PEER_HEADS):
                    n1 = [bcn_ref[tb, hh, al] for al in range(n_a)]
                    e1 = [bce_ref[tb, hh, al] for al in range(n_a)]
                    for k in range(nb_group):
                        rows = slice((nq * nb_group + k) * pack, (nq * nb_group + k + 1) * pack)
                        r2 = r2b_ref[hh, tb, rows, :]
                        e2 = e2b_ref[hh, tb, rows, :]
                        for al in range(n_a):
                            w[k][al] = w[k][al] + jnp.where(r2 < n1[al], e2, zero_b) * e1[al]
                for k in range(nb_group):
                    for al in range(n_a):
                        r0 = al * n_sub + (nq * nb_group + k) * pack
                        xa = act_ref[r0:r0 + pack, lanes]
                        g = 0.5 * xa * (1.0 + lax.erf(xa * math.sqrt(0.5)))
                        p_ref[r0:r0 + pack, lanes] = g.astype(BF16) * w[k][al]
        acc_ref[:, cl] += _dot(vt_ref[...], p_ref[:, cl])

    @pl.when(j == pl.num_programs(1) - 1)
    def _finish():
        gate2 = mod[:, 5 * d:6 * d]
        x2 = x_ref[...] + gate2 * acc_ref[...].T
        o_ref[...] = _rms(x2, fw_ref[...])


def _peer(x1, mod3, seq, norm2_w, wq_b, keys_b, u_b, vt_b, final_w):
    t, d = x1.shape
    e = u_b.shape[0]
    qcols = wq_b.shape[1]
    tiles_per_seq = seq // PEER_TM
    n_tb = PEER_TM // LANES
    const2 = lambda i, j: (0, 0)
    return pl.pallas_call(
        _peer_kernel,
        grid=(t // PEER_TM, e // PEER_TE),
        in_specs=[
            pl.BlockSpec((PEER_TM, d), lambda i, j: (i, 0)),
            pl.BlockSpec((1, 1, mod3.shape[2]), lambda i, j: (i // tiles_per_seq, 0, 0)),
            pl.BlockSpec((1, d), const2),
            pl.BlockSpec((d, qcols), const2, pipeline_mode=pl.Buffered(1)),
            pl.BlockSpec(keys_b.shape, lambda i, j: (0, 0, 0), pipeline_mode=pl.Buffered(1)),
            pl.BlockSpec((PEER_TE, d), lambda i, j: (j, 0)),
            pl.BlockSpec((d, PEER_TE), lambda i, j: (0, j)),
            pl.BlockSpec((1, d), const2),
        ],
        out_specs=pl.BlockSpec((PEER_TM, d), lambda i, j: (i, 0)),
        out_shape=jax.ShapeDtypeStruct((t, d), F32),
        scratch_shapes=[
            pltpu.VMEM((d, PEER_TM), BF16),
            pltpu.VMEM((qcols, PEER_TM), BF16),
            pltpu.VMEM((d, PEER_TM), F32),
            pltpu.VMEM((2 * PEER_HEADS, HEAD_DIM, LANES), F32),
            pltpu.VMEM((2 * PEER_HEADS, HEAD_DIM, LANES), F32),
            pltpu.VMEM((2, PEER_TOPK, PEER_HEADS, LANES), F32),
            pltpu.VMEM((PEER_HEADS, n_tb, HEAD_DIM, LANES), F32),
            pltpu.VMEM((PEER_HEADS, n_tb, HEAD_DIM, LANES), F32),
            pltpu.VMEM((PEER_HEADS, n_tb, HEAD_DIM, LANES), BF16),
            pltpu.VMEM((PEER_HEADS, n_tb, HEAD_DIM, LANES), BF16),
            pltpu.VMEM((n_tb, PEER_HEADS, SUBLANES, 2 * SUBLANES, LANES), BF16),
            pltpu.VMEM((n_tb, PEER_HEADS, SUBLANES, 2 * SUBLANES, LANES), BF16),
            pltpu.VMEM((PEER_TE, PEER_TM), F32),
            pltpu.VMEM((PEER_TE, PEER_TM), BF16),
        ],
        compiler_params=pltpu.CompilerParams(
            dimension_semantics=("arbitrary", "arbitrary"), vmem_limit_bytes=VMEM_LIMIT_BYTES),
        name="peer_dense",
    )(x1, mod3, norm2_w, wq_b, keys_b, u_b, vt_b, final_w)


def kernel(x, c, w_ada, b_ada, norm1_w, w_in, hgrn_lb_logits, hgrn_onorm_w, conv_w, conv_onorm_w,
           w_out, norm2_w, peer_w_query, peer_sub_keys, peer_u, peer_v, final_norm_w):
    bsz, seq, d = x.shape
    assert w_ada.shape[0] == 1, "single-layer kernel"
    assert seq % MIX_TS == 0 and seq % PEER_TM == 0 and MIX_TS % CHUNK == 0
    assert peer_u.shape[1] % PEER_TE == 0 and PEER_TE % HEAD_DIM == 0

    c_pad = jnp.pad(c, ((0, SUBLANES - bsz), (0, 0)))
    mod = _modulation(c_pad, w_ada[0], b_ada[0][None, :])
    mod3 = mod[:bsz].reshape(bsz, 1, mod.shape[1])

    u_b, vt_b = _prep_tables(peer_u[0], peer_v[0])

    x1 = _mixer(x, mod3, norm1_w[0][None, :], w_in[0].astype(BF16), hgrn_lb_logits,
                hgrn_onorm_w[0][None, :], conv_w[0], conv_onorm_w[0][None, :], w_out[0].astype(BF16))

    keys_b = peer_sub_keys[0].reshape(2 * PEER_HEADS, HEAD_DIM, HEAD_DIM).astype(BF16)
    out = _peer(x1.reshape(bsz * seq, d), mod3, seq, norm2_w[0][None, :],
                peer_w_query[0].astype(BF16), keys_b, u_b, vt_b, final_norm_w[None, :])
    return out.reshape(bsz, seq, d)
```

```python
import functools
import math

import jax
import jax.numpy as jnp
from jax import lax
from jax.experimental import pallas as pl
from jax.experimental.pallas import tpu as pltpu

F32 = jnp.float32
BF16 = jnp.bfloat16
EPS = 1e-6

LANES = 128
SUBLANES = 8
VMEM_LIMIT_BYTES = 56 * 1024 * 1024

HEAD_DIM = 128
HGRN_HEADS = 4
CONV_GROUPS = 4
PEER_HEADS = 8
PEER_TOPK = 16
LAYER = 0

MOD_TN = 1536
PREP_TE = 512
MIX_TS = 512
CHUNK = 128
DIAG = 8
PEER_TM = 512
PEER_TE = HEAD_DIM * SUBLANES
PEER_CHUNK_KEYS = 2


def _dot(a, b):
    return jnp.dot(a, b, preferred_element_type=F32)


def _dot_nt(a, b):
    return lax.dot_general(a, b, (((1,), (1,)), ((), ())), preferred_element_type=F32)


def _dot_tn(a, b):
    return lax.dot_general(a, b, (((0,), (0,)), ((), ())), preferred_element_type=F32)


def _split3(x):
    hi = x.astype(BF16)
    r1 = x - hi.astype(F32)
    mid = r1.astype(BF16)
    lo = (r1 - mid.astype(F32)).astype(BF16)
    return hi, mid, lo


def _rms(x, w):
    ms = jnp.mean(x * x, axis=-1, keepdims=True)
    return x * lax.rsqrt(ms + EPS) * w


def _mod_kernel(c_ref, w_ref, b_ref, o_ref):
    c = c_ref[...]
    ca = c * jax.nn.sigmoid(c)
    w = w_ref[...]
    ca_hi = ca.astype(BF16)
    ca_lo = (ca - ca_hi.astype(F32)).astype(BF16)
    w_hi = w.astype(BF16)
    w_lo = (w - w_hi.astype(F32)).astype(BF16)
    acc = _dot(ca_hi, w_hi) + _dot(ca_lo, w_hi) + _dot(ca_hi, w_lo)
    o_ref[...] = acc + b_ref[...]


def _modulation(c_pad, w_ada, b_ada):
    rows, d = c_pad.shape
    n = w_ada.shape[1]
    return pl.pallas_call(
        _mod_kernel,
        grid=(n // MOD_TN,),
        in_specs=[
            pl.BlockSpec((rows, d), lambda j: (0, 0)),
            pl.BlockSpec((d, MOD_TN), lambda j: (0, j)),
            pl.BlockSpec((1, MOD_TN), lambda j: (0, j)),
        ],
        out_specs=pl.BlockSpec((rows, MOD_TN), lambda j: (0, j)),
        out_shape=jax.ShapeDtypeStruct((rows, n), F32),
        compiler_params=pltpu.CompilerParams(
            dimension_semantics=("arbitrary",), vmem_limit_bytes=VMEM_LIMIT_BYTES),
        name="adaln_mod",
    )(c_pad, w_ada, b_ada)


def _prep_kernel(u_ref, v_ref, ub_ref, vt_ref):
    ub_ref[...] = u_ref[...].astype(BF16)
    vt_ref[...] = v_ref[...].T.astype(BF16)


def _prep_tables(peer_u, peer_v):
    e, d = peer_u.shape
    return pl.pallas_call(
        _prep_kernel,
        grid=(e // PREP_TE,),
        in_specs=[
            pl.BlockSpec((PREP_TE, d), lambda j: (j, 0)),
            pl.BlockSpec((PREP_TE, d), lambda j: (j, 0)),
        ],
        out_specs=[
            pl.BlockSpec((PREP_TE, d), lambda j: (j, 0)),
            pl.BlockSpec((d, PREP_TE), lambda j: (0, j)),
        ],
        out_shape=[
            jax.ShapeDtypeStruct((e, d), BF16),
            jax.ShapeDtypeStruct((d, e), BF16),
        ],
        compiler_params=pltpu.CompilerParams(
            dimension_semantics=("arbitrary",), vmem_limit_bytes=VMEM_LIMIT_BYTES),
        name="peer_tables",
    )(peer_u, peer_v)


def _mixer_kernel(x_ref, mod_ref, n1w_ref, win_ref, lbl_ref, honw_ref, cw_ref, conw_ref, wout_ref,
                  o_ref, proj_ref, st_ref, ubuf_ref, ycat_ref, kk_ref, bb_ref, sc_ref):
    ts, d = x_ref.shape[1], x_ref.shape[2]
    hw = HGRN_HEADS * HEAD_DIM
    cwid = CONV_GROUPS * HEAD_DIM
    n_chunks = ts // CHUNK

    @pl.when(pl.program_id(1) == 0)
    def _():
        st_ref[...] = jnp.zeros_like(st_ref)
        ubuf_ref[0:SUBLANES, :] = jnp.zeros((SUBLANES, cwid), F32)

    x = x_ref[0]
    mod = mod_ref[0]
    shift1 = mod[:, 0:d]
    scale1 = mod[:, d:2 * d]
    gate1 = mod[:, 2 * d:3 * d]
    h = _rms(x, n1w_ref[...]) * (1.0 + scale1) + shift1
    proj_ref[...] = _dot(h.astype(BF16), win_ref[...])

    lbl = lbl_ref[...]
    lbe = jnp.exp(lbl - jnp.max(lbl, axis=0, keepdims=True))
    lbp = lbe / jnp.sum(lbe, axis=0, keepdims=True)
    lb = jnp.sum(lbp[0:LAYER + 1, :], axis=0, keepdims=True)

    row = lax.broadcasted_iota(jnp.int32, (CHUNK, CHUNK), 0)
    col = lax.broadcasted_iota(jnp.int32, (CHUNK, CHUNK), 1)
    tril = jnp.where(col <= row, 1.0, 0.0).astype(BF16)
    rowd = lax.broadcasted_iota(jnp.int32, (CHUNK, HEAD_DIM), 0)
    sub8 = lax.broadcasted_iota(jnp.int32, (DIAG, HEAD_DIM), 0)
    lane8 = lax.broadcasted_iota(jnp.int32, (DIAG, CHUNK), 1)
    halves = []
    hs = CHUNK // 2
    while hs >= DIAG:
        halves.append(hs)
        hs //= 2

    def chunk_body(c, carry):
        r0 = pl.multiple_of(c * CHUNK, CHUNK)
        fraw = proj_ref[pl.ds(r0, CHUNK), hw:2 * hw]
        f = lb + (1.0 - lb) * jax.nn.sigmoid(fraw)
        logf = jnp.log(f)
        kk = (1.0 - lb) * jax.nn.sigmoid(-fraw)
        hi, mid, lo = _split3(logf)
        b_all = _dot(tril, hi) + _dot(tril, mid) + _dot(tril, lo)
        kk_ref[...] = kk
        bb_ref[...] = b_all

        def diag_body(j, carry2):
            rr = pl.multiple_of(j * DIAG, DIAG)
            for hd in range(HGRN_HEADS):
                sl = slice(hd * HEAD_DIM, (hd + 1) * HEAD_DIM)
                qb = proj_ref[pl.ds(r0 + rr, DIAG), sl]
                kb = kk_ref[pl.ds(rr, DIAG), sl]
                bb = bb_ref[pl.ds(rr, DIAG), sl]
                slab = jnp.zeros((DIAG, CHUNK), F32)
                for s in range(DIAG):
                    arg = jnp.where(sub8 >= s, bb - bb[s:s + 1, :], -1e30)
                    p = qb * kb[s:s + 1, :] * jnp.exp(arg)
                    colsum = jnp.sum(p, axis=-1, keepdims=True)
                    slab = jnp.where(lane8 == rr + s, colsum, slab)
                sc_ref[hd, pl.ds(rr, DIAG), :] = slab
            return carry2

        lax.fori_loop(0, CHUNK // DIAG, diag_body, 0)

        for hd in range(HGRN_HEADS):
            sl = slice(hd * HEAD_DIM, (hd + 1) * HEAD_DIM)
            q = proj_ref[pl.ds(r0, CHUNK), sl]
            v = proj_ref[pl.ds(r0, CHUNK), 2 * hw + hd * HEAD_DIM:2 * hw + (hd + 1) * HEAD_DIM]
            gout = proj_ref[pl.ds(r0, CHUNK), 3 * hw + hd * HEAD_DIM:3 * hw + (hd + 1) * HEAD_DIM]
            k = kk[:, sl]
            b = b_all[:, sl]
            scores = sc_ref[hd]
            for hs in halves:
                parts = []
                for blk in range(CHUNK // (2 * hs)):
                    r = blk * 2 * hs + hs - 1
                    parts.append(jnp.broadcast_to(b[r:r + 1, :], (2 * hs, HEAD_DIM)))
                bref = parts[0] if len(parts) == 1 else jnp.concatenate(parts, axis=0)
                dd = b - bref
                second = ((rowd // hs) % 2) == 1
                e = jnp.exp(jnp.where(second, dd, -dd))
                ql = jnp.where(second, q * e, 0.0)
                kl = jnp.where(second, 0.0, k * e)
                sl_scores = _dot_nt(ql.astype(BF16), kl.astype(BF16))
                valid = ((row // (2 * hs)) == (col // (2 * hs))) & (((row // hs) % 2) == 1) & (((col // hs) % 2) == 0)
                scores = jnp.where(valid, sl_scores, scores)
            st = st_ref[hd]
            o = _dot(scores.astype(BF16), v.astype(BF16))
            o = o + _dot_nt((q * jnp.exp(b)).astype(BF16), st.astype(BF16))
            b_end = b[CHUNK - 1:CHUNK, :]
            ks = k * jnp.exp(b_end - b)
            st_ref[hd] = st * jnp.exp(b_end) + _dot_tn(v.astype(BF16), ks.astype(BF16))
            oh = _rms(o, honw_ref[:, sl]) * (gout * jax.nn.sigmoid(gout))
            ycat_ref[pl.ds(r0, CHUNK), sl] = oh
        return carry

    lax.fori_loop(0, n_chunks, chunk_body, 0)

    bg = proj_ref[:, 4 * hw:4 * hw + cwid]
    cg = proj_ref[:, 4 * hw + cwid:4 * hw + 2 * cwid]
    xin = proj_ref[:, 4 * hw + 2 * cwid:4 * hw + 3 * cwid]
    u = cg * xin
    ubuf_ref[SUBLANES:SUBLANES + ts, :] = u
    cw = cw_ref[...]
    y = (cw[0:1, :] * ubuf_ref[SUBLANES - 2:SUBLANES - 2 + ts, :]
         + cw[1:2, :] * ubuf_ref[SUBLANES - 1:SUBLANES - 1 + ts, :]
         + cw[2:3, :] * u)
    ubuf_ref[0:SUBLANES, :] = u[ts - SUBLANES:ts, :]
    y = bg * y
    for g in range(CONV_GROUPS):
        sl = slice(g * HEAD_DIM, (g + 1) * HEAD_DIM)
        ycat_ref[:, hw + g * HEAD_DIM:hw + (g + 1) * HEAD_DIM] = _rms(y[:, sl], conw_ref[:, sl])

    mix = _dot(ycat_ref[...].astype(BF16), wout_ref[...])
    o_ref[0] = x + gate1 * mix


def _mixer(x, mod3, norm1_w, w_in_b, lb_logits, honw, conv_w, conw, w_out_b):
    bsz, seq, d = x.shape
    ncols = w_in_b.shape[1]
    hw = HGRN_HEADS * HEAD_DIM
    cwid = CONV_GROUPS * HEAD_DIM
    const2 = lambda b, s: (0, 0)
    return pl.pallas_call(
        _mixer_kernel,
        grid=(bsz, seq // MIX_TS),
        in_specs=[
            pl.BlockSpec((1, MIX_TS, d), lambda b, s: (b, s, 0)),
            pl.BlockSpec((1, 1, mod3.shape[2]), lambda b, s: (b, 0, 0)),
            pl.BlockSpec((1, d), const2),
            pl.BlockSpec((d, ncols), const2, pipeline_mode=pl.Buffered(1)),
            pl.BlockSpec(lb_logits.shape, const2),
            pl.BlockSpec((1, hw), const2),
            pl.BlockSpec(conv_w.shape, const2),
            pl.BlockSpec((1, cwid), const2),
            pl.BlockSpec((hw + cwid, d), const2, pipeline_mode=pl.Buffered(1)),
        ],
        out_specs=pl.BlockSpec((1, MIX_TS, d), lambda b, s: (b, s, 0)),
        out_shape=jax.ShapeDtypeStruct((bsz, seq, d), F32),
        scratch_shapes=[
            pltpu.VMEM((MIX_TS, ncols), F32),
            pltpu.VMEM((HGRN_HEADS, HEAD_DIM, HEAD_DIM), F32),
            pltpu.VMEM((SUBLANES + MIX_TS, cwid), F32),
            pltpu.VMEM((MIX_TS, hw + cwid), F32),
            pltpu.VMEM((CHUNK, hw), F32),
            pltpu.VMEM((CHUNK, hw), F32),
            pltpu.VMEM((HGRN_HEADS, CHUNK, CHUNK), F32),
        ],
        compiler_params=pltpu.CompilerParams(
            dimension_semantics=("arbitrary", "arbitrary"), vmem_limit_bytes=VMEM_LIMIT_BYTES),
        name="token_mixer",
    )(x, mod3, norm1_w, w_in_b, lb_logits, honw, conv_w, conw, w_out_b)


def _candidates():
    return [(i, j) for i in range(PEER_TOPK) for j in range(PEER_TOPK)
            if (i + 1) * (j + 1) <= PEER_TOPK]


def _peer_kernel(x_ref, mod_ref, n2w_ref, wq_ref, keys_ref, u_ref, vt_ref, fw_ref,
                 o_ref, h2t_ref, qt_ref, acc_ref, st_ref, rk_ref, vals_ref,
                 n1f_ref, e1f_ref, r2f_ref, e2f_ref, bcn_ref, bce_ref, act_ref, p_ref):
    tm, d = x_ref.shape
    j = pl.program_id(1)
    n_tb = tm // LANES
    n_sub = HEAD_DIM
    mod = mod_ref[0]

    @pl.when(j == 0)
    def _route():
        shift2 = mod[:, 3 * d:4 * d]
        scale2 = mod[:, 4 * d:5 * d]
        h2 = _rms(x_ref[...], n2w_ref[...]) * (1.0 + scale2) + shift2
        h2b = h2.astype(BF16)
        h2t_ref[...] = h2.T.astype(BF16)
        qt_ref[...] = _dot(h2b, wq_ref[...]).T.astype(BF16)
        acc_ref[...] = jnp.zeros_like(acc_ref)

        n_iota = lax.broadcasted_iota(jnp.int32, (n_sub, LANES), 0).astype(F32)
        head_iota = lax.broadcasted_iota(jnp.int32, (PEER_HEADS, LANES), 0)
        cands = _candidates()

        def tb_body(tb, carry):
            lanes = pl.ds(pl.multiple_of(tb * LANES, LANES), LANES)

            def extract(hp, hh, pp):
                qrows = pl.ds(pl.multiple_of(hp * HEAD_DIM, HEAD_DIM), HEAD_DIM)
                s = _dot(keys_ref[hp], qt_ref[qrows, lanes])
                st_ref[hp] = s
                rk_ref[hp] = jnp.full((n_sub, LANES), float(PEER_TOPK), F32)
                for i in range(PEER_TOPK):
                    m = jnp.max(s, axis=0, keepdims=True)
                    first = jnp.min(jnp.where(s == m, n_iota, float(n_sub)), axis=0, keepdims=True)
                    hit = n_iota == first
                    rk_ref[hp] = jnp.where(hit, float(i), rk_ref[hp])
                    s = jnp.where(hit, -jnp.inf, s)
                    vals_ref[pp, i] = jnp.where(head_iota == hh, m, vals_ref[pp, i])

            def hh_body(h2i, carry2):
                for dh in range(2):
                    hh = 2 * h2i + dh
                    for pp in range(2):
                        extract(2 * hh + pp, hh, pp)
                return carry2

            lax.fori_loop(0, PEER_HEADS // 2, hh_body, 0)

            v1 = [vals_ref[0, i] for i in range(PEER_TOPK)]
            v2 = [vals_ref[1, i] for i in range(PEER_TOPK)]
            c = [v1[i] + v2[jj] for (i, jj) in cands]
            c00 = c[0]
            cnt = [jnp.zeros((PEER_HEADS, LANES), F32) for _ in range(PEER_TOPK)]
            z = jnp.zeros((PEER_HEADS, LANES), F32)
            big = float(PEER_TOPK * PEER_TOPK)
            for _ in range(PEER_TOPK):
                m = functools.reduce(jnp.maximum, c)
                first = functools.reduce(
                    jnp.minimum,
                    [jnp.where(c[k] == m, float(i * PEER_TOPK + jj), big) for k, (i, jj) in enumerate(cands)])
                z = z + jnp.exp(m - c00)
                for k, (i, jj) in enumerate(cands):
                    hit = first == float(i * PEER_TOPK + jj)
                    cnt[i] = cnt[i] + jnp.where(hit, 1.0, 0.0)
                    c[k] = jnp.where(hit, -jnp.inf, c[k])
            zinv = 1.0 / z

            for hh in range(PEER_HEADS):
                rank1 = rk_ref[2 * hh]
                n1 = jnp.zeros((n_sub, LANES), F32)
                for i in range(PEER_TOPK):
                    n1 = jnp.where(rank1 == float(i), cnt[i][hh:hh + 1, :], n1)
                n1f_ref[hh, tb] = n1
                e1f_ref[hh, tb] = jnp.exp(st_ref[2 * hh] - v1[0][hh:hh + 1, :])
                r2f_ref[hh, tb] = rk_ref[2 * hh + 1]
                e2f_ref[hh, tb] = jnp.exp(st_ref[2 * hh + 1] - v2[0][hh:hh + 1, :]) * zinv[hh:hh + 1, :]
            return carry

        lax.fori_loop(0, n_tb, tb_body, 0)

    n_a = PEER_TE // n_sub
    assert n_a == SUBLANES and n_a % PEER_CHUNK_KEYS == 0
    arows = pl.ds(pl.multiple_of(j * n_a, SUBLANES), SUBLANES)
    n_chunks = n_a // PEER_CHUNK_KEYS
    crows = PEER_CHUNK_KEYS * n_sub
    grp = 4 * SUBLANES

    for q in range(n_chunks):
        rq = slice(q * crows, (q + 1) * crows)
        act_ref[rq, :] = _dot(u_ref[rq, :], h2t_ref[...])

    for q in range(n_chunks):
        rq = slice(q * crows, (q + 1) * crows)
        als = range(q * PEER_CHUNK_KEYS, (q + 1) * PEER_CHUNK_KEYS)
        for tb in range(n_tb):
            lanes = slice(tb * LANES, (tb + 1) * LANES)
            for hh in range(PEER_HEADS):
                n1t = n1f_ref[hh, tb, arows, :]
                e1t = e1f_ref[hh, tb, arows, :]
                for al in als:
                    bcn_ref[tb, hh, al] = jnp.broadcast_to(n1t[al:al + 1, :], (SUBLANES, LANES))
                    bce_ref[tb, hh, al] = jnp.broadcast_to(e1t[al:al + 1, :], (SUBLANES, LANES))
            for bq in range(n_sub // grp):
                rows = slice(bq * grp, (bq + 1) * grp)
                w = {al: jnp.zeros((grp, LANES), F32) for al in als}
                for hh in range(PEER_HEADS):
                    r2 = r2f_ref[hh, tb, rows, :]
                    e2 = e2f_ref[hh, tb, rows, :]
                    for al in als:
                        n1 = jnp.concatenate([bcn_ref[tb, hh, al]] * (grp // SUBLANES), axis=0)
                        e1 = jnp.concatenate([bce_ref[tb, hh, al]] * (grp // SUBLANES), axis=0)
                        w[al] = w[al] + jnp.where(r2 < n1, e2, 0.0) * e1
                for al in als:
                    r0 = al * n_sub + bq * grp
                    xa = act_ref[r0:r0 + grp, lanes]
                    g = 0.5 * xa * (1.0 + lax.erf(xa * math.sqrt(0.5)))
                    p_ref[r0:r0 + grp, lanes] = (g * w[al]).astype(BF16)
        acc_ref[...] += _dot(vt_ref[:, rq], p_ref[rq, :])

    @pl.when(j == pl.num_programs(1) - 1)
    def _finish():
        gate2 = mod[:, 5 * d:6 * d]
        x2 = x_ref[...] + gate2 * acc_ref[...].T
        o_ref[...] = _rms(x2, fw_ref[...])


def _peer(x1, mod3, seq, norm2_w, wq_b, keys_b, u_b, vt_b, final_w):
    t, d = x1.shape
    e = u_b.shape[0]
    qcols = wq_b.shape[1]
    tiles_per_seq = seq // PEER_TM
    n_tb = PEER_TM // LANES
    const2 = lambda i, j: (0, 0)
    return pl.pallas_call(
        _peer_kernel,
        grid=(t // PEER_TM, e // PEER_TE),
        in_specs=[
            pl.BlockSpec((PEER_TM, d), lambda i, j: (i, 0)),
            pl.BlockSpec((1, 1, mod3.shape[2]), lambda i, j: (i // tiles_per_seq, 0, 0)),
            pl.BlockSpec((1, d), const2),
            pl.BlockSpec((d, qcols), const2, pipeline_mode=pl.Buffered(1)),
            pl.BlockSpec(keys_b.shape, lambda i, j: (0, 0, 0), pipeline_mode=pl.Buffered(1)),
            pl.BlockSpec((PEER_TE, d), lambda i, j: (j, 0)),
            pl.BlockSpec((d, PEER_TE), lambda i, j: (0, j)),
            pl.BlockSpec((1, d), const2),
        ],
        out_specs=pl.BlockSpec((PEER_TM, d), lambda i, j: (i, 0)),
        out_shape=jax.ShapeDtypeStruct((t, d), F32),
        scratch_shapes=[
            pltpu.VMEM((d, PEER_TM), BF16),
            pltpu.VMEM((qcols, PEER_TM), BF16),
            pltpu.VMEM((d, PEER_TM), F32),
            pltpu.VMEM((2 * PEER_HEADS, HEAD_DIM, LANES), F32),
            pltpu.VMEM((2 * PEER_HEADS, HEAD_DIM, LANES), F32),
            pltpu.VMEM((2, PEER_TOPK, PEER_HEADS, LANES), F32),
            pltpu.VMEM((PEER_HEADS, n_tb, HEAD_DIM, LANES), F32),
            pltpu.VMEM((PEER_HEADS, n_tb, HEAD_DIM, LANES), F32),
            pltpu.VMEM((PEER_HEADS, n_tb, HEAD_DIM, LANES), F32),
            pltpu.VMEM((PEER_HEADS, n_tb, HEAD_DIM, LANES), F32),
            pltpu.VMEM((n_tb, PEER_HEADS, SUBLANES, SUBLANES, LANES), F32),
            pltpu.VMEM((n_tb, PEER_HEADS, SUBLANES, SUBLANES, LANES), F32),
            pltpu.VMEM((PEER_TE, PEER_TM), F32),
            pltpu.VMEM((PEER_TE, PEER_TM), BF16),
        ],
        compiler_params=pltpu.CompilerParams(
            dimension_semantics=("arbitrary", "arbitrary"), vmem_limit_bytes=VMEM_LIMIT_BYTES),
        name="peer_dense",
    )(x1, mod3, norm2_w, wq_b, keys_b, u_b, vt_b, final_w)


def kernel(x, c, w_ada, b_ada, norm1_w, w_in, hgrn_lb_logits, hgrn_onorm_w, conv_w, conv_onorm_w,
           w_out, norm2_w, peer_w_query, peer_sub_keys, peer_u, peer_v, final_norm_w):
    bsz, seq, d = x.shape
    assert w_ada.shape[0] == 1, "single-layer kernel"
    assert seq % MIX_TS == 0 and seq % PEER_TM == 0 and MIX_TS % CHUNK == 0
    assert peer_u.shape[1] % PEER_TE == 0 and PEER_TE % HEAD_DIM == 0

    c_pad = jnp.pad(c, ((0, SUBLANES - bsz), (0, 0)))
    mod = _modulation(c_pad, w_ada[0], b_ada[0][None, :])
    mod3 = mod[:bsz].reshape(bsz, 1, mod.shape[1])

    u_b, vt_b = _prep_tables(peer_u[0], peer_v[0])

    x1 = _mixer(x, mod3, norm1_w[0][None, :], w_in[0].astype(BF16), hgrn_lb_logits,
                hgrn_onorm_w[0][None, :], conv_w[0], conv_onorm_w[0][None, :], w_out[0].astype(BF16))

    keys_b = peer_sub_keys[0].reshape(2 * PEER_HEADS, HEAD_DIM, HEAD_DIM).astype(BF16)
    out = _peer(x1.reshape(bsz * seq, d), mod3, seq, norm2_w[0][None, :],
                peer_w_query[0].astype(BF16), keys_b, u_b, vt_b, final_norm_w[None, :])
    return out.reshape(bsz, seq, d)
```

```python
import functools
import math

import jax
import jax.numpy as jnp
from jax import lax
from jax.experimental import pallas as pl
from jax.experimental.pallas import tpu as pltpu

F32 = jnp.float32
BF16 = jnp.bfloat16
EPS = 1e-6

LANES = 128
SUBLANES = 8
VMEM_LIMIT_BYTES = 56 * 1024 * 1024

HEAD_DIM = 128
HGRN_HEADS = 4
CONV_GROUPS = 4
PEER_HEADS = 8
PEER_TOPK = 16
LAYER = 0

MOD_TN = 1536
PREP_TE = 512
MIX_TS = 512
CHUNK = 128
DIAG = 8
PEER_TM = 512
PEER_TE = HEAD_DIM * SUBLANES
GATE_BLOCK_A = 4
GATE_BLOCK_B = 4


def _dot(a, b):
    return jnp.dot(a, b, preferred_element_type=F32)


def _dot_nt(a, b):
    return lax.dot_general(a, b, (((1,), (1,)), ((), ())), preferred_element_type=F32)


def _dot_tn(a, b):
    return lax.dot_general(a, b, (((0,), (0,)), ((), ())), preferred_element_type=F32)


def _split3(x):
    hi = x.astype(BF16)
    r1 = x - hi.astype(F32)
    mid = r1.astype(BF16)
    lo = (r1 - mid.astype(F32)).astype(BF16)
    return hi, mid, lo


def _rms(x, w):
    ms = jnp.mean(x * x, axis=-1, keepdims=True)
    return x * lax.rsqrt(ms + EPS) * w


def _mod_kernel(c_ref, w_ref, b_ref, o_ref):
    c = c_ref[...]
    ca = c * jax.nn.sigmoid(c)
    w = w_ref[...]
    ca_hi = ca.astype(BF16)
    ca_lo = (ca - ca_hi.astype(F32)).astype(BF16)
    w_hi = w.astype(BF16)
    w_lo = (w - w_hi.astype(F32)).astype(BF16)
    acc = _dot(ca_hi, w_hi) + _dot(ca_lo, w_hi) + _dot(ca_hi, w_lo)
    o_ref[...] = acc + b_ref[...]


def _modulation(c_pad, w_ada, b_ada):
    rows, d = c_pad.shape
    n = w_ada.shape[1]
    return pl.pallas_call(
        _mod_kernel,
        grid=(n // MOD_TN,),
        in_specs=[
            pl.BlockSpec((rows, d), lambda j: (0, 0)),
            pl.BlockSpec((d, MOD_TN), lambda j: (0, j)),
            pl.BlockSpec((1, MOD_TN), lambda j: (0, j)),
        ],
        out_specs=pl.BlockSpec((rows, MOD_TN), lambda j: (0, j)),
        out_shape=jax.ShapeDtypeStruct((rows, n), F32),
        compiler_params=pltpu.CompilerParams(
            dimension_semantics=("arbitrary",), vmem_limit_bytes=VMEM_LIMIT_BYTES),
        name="adaln_mod",
    )(c_pad, w_ada, b_ada)


def _prep_kernel(u_ref, v_ref, ub_ref, vt_ref):
    ub_ref[...] = u_ref[...].astype(BF16)
    vt_ref[...] = v_ref[...].T.astype(BF16)


def _prep_tables(peer_u, peer_v):
    e, d = peer_u.shape
    return pl.pallas_call(
        _prep_kernel,
        grid=(e // PREP_TE,),
        in_specs=[
            pl.BlockSpec((PREP_TE, d), lambda j: (j, 0)),
            pl.BlockSpec((PREP_TE, d), lambda j: (j, 0)),
        ],
        out_specs=[
            pl.BlockSpec((PREP_TE, d), lambda j: (j, 0)),
            pl.BlockSpec((d, PREP_TE), lambda j: (0, j)),
        ],
        out_shape=[
            jax.ShapeDtypeStruct((e, d), BF16),
            jax.ShapeDtypeStruct((d, e), BF16),
        ],
        compiler_params=pltpu.CompilerParams(
            dimension_semantics=("arbitrary",), vmem_limit_bytes=VMEM_LIMIT_BYTES),
        name="peer_tables",
    )(peer_u, peer_v)


def _mixer_kernel(x_ref, mod_ref, n1w_ref, win_ref, lbl_ref, honw_ref, cw_ref, conw_ref, wout_ref,
                  o_ref, proj_ref, st_ref, ubuf_ref, ycat_ref, kk_ref, bb_ref, sc_ref):
    ts, d = x_ref.shape[1], x_ref.shape[2]
    hw = HGRN_HEADS * HEAD_DIM
    cwid = CONV_GROUPS * HEAD_DIM
    n_chunks = ts // CHUNK

    @pl.when(pl.program_id(1) == 0)
    def _():
        st_ref[...] = jnp.zeros_like(st_ref)
        ubuf_ref[0:SUBLANES, :] = jnp.zeros((SUBLANES, cwid), F32)

    x = x_ref[0]
    mod = mod_ref[0]
    shift1 = mod[:, 0:d]
    scale1 = mod[:, d:2 * d]
    gate1 = mod[:, 2 * d:3 * d]
    h = _rms(x, n1w_ref[...]) * (1.0 + scale1) + shift1
    proj_ref[...] = _dot(h.astype(BF16), win_ref[...])

    lbl = lbl_ref[...]
    lbe = jnp.exp(lbl - jnp.max(lbl, axis=0, keepdims=True))
    lbp = lbe / jnp.sum(lbe, axis=0, keepdims=True)
    lb = jnp.sum(lbp[0:LAYER + 1, :], axis=0, keepdims=True)

    row = lax.broadcasted_iota(jnp.int32, (CHUNK, CHUNK), 0)
    col = lax.broadcasted_iota(jnp.int32, (CHUNK, CHUNK), 1)
    tril = jnp.where(col <= row, 1.0, 0.0).astype(BF16)
    rowd = lax.broadcasted_iota(jnp.int32, (CHUNK, HEAD_DIM), 0)
    sub8 = lax.broadcasted_iota(jnp.int32, (DIAG, HEAD_DIM), 0)
    lane8 = lax.broadcasted_iota(jnp.int32, (DIAG, CHUNK), 1)
    halves = []
    hs = CHUNK // 2
    while hs >= DIAG:
        halves.append(hs)
        hs //= 2

    def chunk_body(c, carry):
        r0 = pl.multiple_of(c * CHUNK, CHUNK)
        fraw = proj_ref[pl.ds(r0, CHUNK), hw:2 * hw]
        f = lb + (1.0 - lb) * jax.nn.sigmoid(fraw)
        logf = jnp.log(f)
        kk = (1.0 - lb) * jax.nn.sigmoid(-fraw)
        hi, mid, lo = _split3(logf)
        b_all = _dot(tril, hi) + _dot(tril, mid) + _dot(tril, lo)
        kk_ref[...] = kk
        bb_ref[...] = b_all

        def diag_body(j, carry2):
            for half in range(2):
                diag_block(pl.multiple_of((2 * j + half) * DIAG, DIAG))
            return carry2

        def diag_block(rr):
            for hd in range(HGRN_HEADS):
                sl = slice(hd * HEAD_DIM, (hd + 1) * HEAD_DIM)
                qb = proj_ref[pl.ds(r0 + rr, DIAG), sl]
                kb = kk_ref[pl.ds(rr, DIAG), sl]
                bb = bb_ref[pl.ds(rr, DIAG), sl]
                slab = jnp.zeros((DIAG, CHUNK), F32)
                for s in range(DIAG):
                    arg = jnp.where(sub8 >= s, bb - bb[s:s + 1, :], -1e30)
                    p = qb * kb[s:s + 1, :] * jnp.exp(arg)
                    colsum = jnp.sum(p, axis=-1, keepdims=True)
                    slab = jnp.where(lane8 == rr + s, colsum, slab)
                sc_ref[hd, pl.ds(rr, DIAG), :] = slab

        lax.fori_loop(0, CHUNK // (2 * DIAG), diag_body, 0)

        for hd in range(HGRN_HEADS):
            sl = slice(hd * HEAD_DIM, (hd + 1) * HEAD_DIM)
            q = proj_ref[pl.ds(r0, CHUNK), sl]
            v = proj_ref[pl.ds(r0, CHUNK), 2 * hw + hd * HEAD_DIM:2 * hw + (hd + 1) * HEAD_DIM]
            gout = proj_ref[pl.ds(r0, CHUNK), 3 * hw + hd * HEAD_DIM:3 * hw + (hd + 1) * HEAD_DIM]
            k = kk[:, sl]
            b = b_all[:, sl]
            scores = sc_ref[hd]
            for hs in halves:
                parts = []
                for blk in range(CHUNK // (2 * hs)):
                    r = blk * 2 * hs + hs - 1
                    parts.append(jnp.broadcast_to(b[r:r + 1, :], (2 * hs, HEAD_DIM)))
                bref = parts[0] if len(parts) == 1 else jnp.concatenate(parts, axis=0)
                dd = b - bref
                second = ((rowd // hs) % 2) == 1
                e = jnp.exp(jnp.where(second, dd, -dd))
                ql = jnp.where(second, q * e, 0.0)
                kl = jnp.where(second, 0.0, k * e)
                sl_scores = _dot_nt(ql.astype(BF16), kl.astype(BF16))
                valid = ((row // (2 * hs)) == (col // (2 * hs))) & (((row // hs) % 2) == 1) & (((col // hs) % 2) == 0)
                scores = jnp.where(valid, sl_scores, scores)
            st = st_ref[hd]
            o = _dot(scores.astype(BF16), v.astype(BF16))
            o = o + _dot_nt((q * jnp.exp(b)).astype(BF16), st.astype(BF16))
            b_end = b[CHUNK - 1:CHUNK, :]
            ks = k * jnp.exp(b_end - b)
            st_ref[hd] = st * jnp.exp(b_end) + _dot_tn(v.astype(BF16), ks.astype(BF16))
            oh = _rms(o, honw_ref[:, sl]) * (gout * jax.nn.sigmoid(gout))
            ycat_ref[pl.ds(r0, CHUNK), sl] = oh
        return carry

    lax.fori_loop(0, n_chunks, chunk_body, 0)

    bg = proj_ref[:, 4 * hw:4 * hw + cwid]
    cg = proj_ref[:, 4 * hw + cwid:4 * hw + 2 * cwid]
    xin = proj_ref[:, 4 * hw + 2 * cwid:4 * hw + 3 * cwid]
    u = cg * xin
    ubuf_ref[SUBLANES:SUBLANES + ts, :] = u
    cw = cw_ref[...]
    y = (cw[0:1, :] * ubuf_ref[SUBLANES - 2:SUBLANES - 2 + ts, :]
         + cw[1:2, :] * ubuf_ref[SUBLANES - 1:SUBLANES - 1 + ts, :]
         + cw[2:3, :] * u)
    ubuf_ref[0:SUBLANES, :] = u[ts - SUBLANES:ts, :]
    y = bg * y
    for g in range(CONV_GROUPS):
        sl = slice(g * HEAD_DIM, (g + 1) * HEAD_DIM)
        ycat_ref[:, hw + g * HEAD_DIM:hw + (g + 1) * HEAD_DIM] = _rms(y[:, sl], conw_ref[:, sl])

    mix = _dot(ycat_ref[...].astype(BF16), wout_ref[...])
    o_ref[0] = x + gate1 * mix


def _mixer(x, mod3, norm1_w, w_in_b, lb_logits, honw, conv_w, conw, w_out_b):
    bsz, seq, d = x.shape
    ncols = w_in_b.shape[1]
    hw = HGRN_HEADS * HEAD_DIM
    cwid = CONV_GROUPS * HEAD_DIM
    const2 = lambda b, s: (0, 0)
    return pl.pallas_call(
        _mixer_kernel,
        grid=(bsz, seq // MIX_TS),
        in_specs=[
            pl.BlockSpec((1, MIX_TS, d), lambda b, s: (b, s, 0)),
            pl.BlockSpec((1, 1, mod3.shape[2]), lambda b, s: (b, 0, 0)),
            pl.BlockSpec((1, d), const2),
            pl.BlockSpec((d, ncols), const2, pipeline_mode=pl.Buffered(1)),
            pl.BlockSpec(lb_logits.shape, const2),
            pl.BlockSpec((1, hw), const2),
            pl.BlockSpec(conv_w.shape, const2),
            pl.BlockSpec((1, cwid), const2),
            pl.BlockSpec((hw + cwid, d), const2, pipeline_mode=pl.Buffered(1)),
        ],
        out_specs=pl.BlockSpec((1, MIX_TS, d), lambda b, s: (b, s, 0)),
        out_shape=jax.ShapeDtypeStruct((bsz, seq, d), F32),
        scratch_shapes=[
            pltpu.VMEM((MIX_TS, ncols), F32),
            pltpu.VMEM((HGRN_HEADS, HEAD_DIM, HEAD_DIM), F32),
            pltpu.VMEM((SUBLANES + MIX_TS, cwid), F32),
            pltpu.VMEM((MIX_TS, hw + cwid), F32),
            pltpu.VMEM((CHUNK, hw), F32),
            pltpu.VMEM((CHUNK, hw), F32),
            pltpu.VMEM((HGRN_HEADS, CHUNK, CHUNK), F32),
        ],
        compiler_params=pltpu.CompilerParams(
            dimension_semantics=("arbitrary", "arbitrary"), vmem_limit_bytes=VMEM_LIMIT_BYTES),
        name="token_mixer",
    )(x, mod3, norm1_w, w_in_b, lb_logits, honw, conv_w, conw, w_out_b)


def _candidates():
    return [(i, j) for i in range(PEER_TOPK) for j in range(PEER_TOPK)
            if (i + 1) * (j + 1) <= PEER_TOPK]


def _peer_kernel(x_ref, mod_ref, n2w_ref, wq_ref, keys_ref, u_ref, vt_ref, fw_ref,
                 o_ref, h2t_ref, qt_ref, acc_ref, st_ref, rk_ref, vals_ref,
                 n1f_ref, e1f_ref, r2b_ref, e2b_ref, bcn_ref, bce_ref, act_ref, p_ref):
    tm, d = x_ref.shape
    j = pl.program_id(1)
    n_tb = tm // LANES
    n_sub = HEAD_DIM
    mod = mod_ref[0]

    @pl.when(j == 0)
    def _route():
        shift2 = mod[:, 3 * d:4 * d]
        scale2 = mod[:, 4 * d:5 * d]
        h2 = _rms(x_ref[...], n2w_ref[...]) * (1.0 + scale2) + shift2
        h2b = h2.astype(BF16)
        h2t_ref[...] = h2.T.astype(BF16)
        qt_ref[...] = _dot(h2b, wq_ref[...]).T.astype(BF16)
        acc_ref[...] = jnp.zeros_like(acc_ref)

        n_iota = lax.broadcasted_iota(jnp.int32, (n_sub, LANES), 0).astype(F32)
        head_iota = lax.broadcasted_iota(jnp.int32, (PEER_HEADS, LANES), 0)
        cands = _candidates()

        def tb_body(tb, carry):
            lanes = pl.ds(pl.multiple_of(tb * LANES, LANES), LANES)

            def extract(hp, hh, pp, break_ties):
                s = st_ref[hp]
                rk_ref[hp] = jnp.full((n_sub, LANES), float(PEER_TOPK), F32)
                for i in range(PEER_TOPK):
                    m = jnp.max(s, axis=0, keepdims=True)
                    if break_ties:
                        first = jnp.min(jnp.where(s == m, n_iota, float(n_sub)), axis=0, keepdims=True)
                        hit = n_iota == first
                    else:
                        hit = s == m
                    rk_ref[hp] = jnp.where(hit, float(i), rk_ref[hp])
                    s = jnp.where(hit, -jnp.inf, s)
                    vals_ref[pp, i] = jnp.where(head_iota == hh, m, vals_ref[pp, i])
                picked = jnp.sum(jnp.where(rk_ref[hp] < float(PEER_TOPK), 1.0, 0.0), axis=0, keepdims=True)
                return jnp.max(jnp.abs(picked - float(PEER_TOPK)))

            def hh_body(h2i, carry2):
                hps = [(2 * (2 * h2i + dh) + pp, 2 * h2i + dh, pp) for dh in range(2) for pp in range(2)]
                for hp, _, _ in hps:
                    qrows = pl.ds(pl.multiple_of(hp * HEAD_DIM, HEAD_DIM), HEAD_DIM)
                    st_ref[hp] = _dot(keys_ref[hp], qt_ref[qrows, lanes])
                excess = [extract(hp, hh, pp, False) for hp, hh, pp in hps]
                tied = functools.reduce(jnp.maximum, excess) > 0.0

                @pl.when(tied)
                def _():
                    for hp, hh, pp in hps:
                        extract(hp, hh, pp, True)

                return carry2

            lax.fori_loop(0, PEER_HEADS // 2, hh_body, 0)

            v1 = [vals_ref[0, i] for i in range(PEER_TOPK)]
            v2 = [vals_ref[1, i] for i in range(PEER_TOPK)]
            c = [v1[i] + v2[jj] for (i, jj) in cands]
            c00 = c[0]
            cnt = [jnp.zeros((PEER_HEADS, LANES), F32) for _ in range(PEER_TOPK)]
            z = jnp.zeros((PEER_HEADS, LANES), F32)
            big = float(PEER_TOPK * PEER_TOPK)
            for _ in range(PEER_TOPK):
                m = functools.reduce(jnp.maximum, c)
                first = functools.reduce(
                    jnp.minimum,
                    [jnp.where(c[k] == m, float(i * PEER_TOPK + jj), big) for k, (i, jj) in enumerate(cands)])
                z = z + jnp.exp(m - c00)
                for k, (i, jj) in enumerate(cands):
                    hit = first == float(i * PEER_TOPK + jj)
                    cnt[i] = cnt[i] + jnp.where(hit, 1.0, 0.0)
                    c[k] = jnp.where(hit, -jnp.inf, c[k])
            zinv = 1.0 / z

            for hh in range(PEER_HEADS):
                rank1 = rk_ref[2 * hh]
                n1 = jnp.zeros((n_sub, LANES), F32)
                for i in range(PEER_TOPK):
                    n1 = jnp.where(rank1 == float(i), cnt[i][hh:hh + 1, :], n1)
                n1f_ref[hh, tb] = n1
                e1f_ref[hh, tb] = jnp.exp(st_ref[2 * hh] - v1[0][hh:hh + 1, :])
                r2b_ref[hh, tb] = rk_ref[2 * hh + 1].astype(BF16)
                e2 = jnp.exp(st_ref[2 * hh + 1] - v2[0][hh:hh + 1, :]) * zinv[hh:hh + 1, :]
                e2b_ref[hh, tb] = e2.astype(BF16)
            return carry

        lax.fori_loop(0, n_tb, tb_body, 0)

    act_ref[...] = _dot(u_ref[...], h2t_ref[...])
    n_a = PEER_TE // n_sub
    assert n_a == SUBLANES
    arows = pl.ds(pl.multiple_of(j * n_a, SUBLANES), SUBLANES)
    pack = 2 * SUBLANES
    zero_b = jnp.zeros((pack, LANES), BF16)
    n_bv = n_sub // pack
    assert n_a % GATE_BLOCK_A == 0 and n_bv % GATE_BLOCK_B == 0

    for tb in range(n_tb):
        lanes = slice(tb * LANES, (tb + 1) * LANES)
        for hh in range(PEER_HEADS):
            n1t = n1f_ref[hh, tb, arows, :]
            e1t = e1f_ref[hh, tb, arows, :]
            for al in range(n_a):
                bcn_ref[tb, hh, al] = jnp.broadcast_to(n1t[al:al + 1, :], (pack, LANES)).astype(BF16)
                bce_ref[tb, hh, al] = jnp.broadcast_to(e1t[al:al + 1, :], (pack, LANES)).astype(BF16)

    nblk_a = n_a // GATE_BLOCK_A
    nblk_b = n_bv // GATE_BLOCK_B

    def gate_block(i, carry):
        tb = i // (nblk_a * nblk_b)
        a0 = ((i // nblk_b) % nblk_a) * GATE_BLOCK_A
        b0 = (i % nblk_b) * GATE_BLOCK_B
        lanes = pl.ds(pl.multiple_of(tb * LANES, LANES), LANES)
        w = [[zero_b for _ in range(GATE_BLOCK_B)] for _ in range(GATE_BLOCK_A)]
        for hh in range(PEER_HEADS):
            n1 = [bcn_ref[tb, hh, a0 + ia] for ia in range(GATE_BLOCK_A)]
            e1 = [bce_ref[tb, hh, a0 + ia] for ia in range(GATE_BLOCK_A)]
            for ib in range(GATE_BLOCK_B):
                rows = pl.ds(pl.multiple_of((b0 + ib) * pack, pack), pack)
                r2 = r2b_ref[hh, tb, rows, :]
                e2 = e2b_ref[hh, tb, rows, :]
                for ia in range(GATE_BLOCK_A):
                    w[ia][ib] = w[ia][ib] + jnp.where(r2 < n1[ia], e2, zero_b) * e1[ia]
        for ia in range(GATE_BLOCK_A):
            for ib in range(GATE_BLOCK_B):
                erows = pl.ds(pl.multiple_of((a0 + ia) * n_sub + (b0 + ib) * pack, pack), pack)
                xa = act_ref[erows, lanes]
                g = 0.5 * xa * (1.0 + lax.erf(xa * math.sqrt(0.5)))
                p_ref[erows, lanes] = g.astype(BF16) * w[ia][ib]
        return carry

    lax.fori_loop(0, n_tb * nblk_a * nblk_b, gate_block, 0)
    acc_ref[...] += _dot(vt_ref[...], p_ref[...])

    @pl.when(j == pl.num_programs(1) - 1)
    def _finish():
        gate2 = mod[:, 5 * d:6 * d]
        x2 = x_ref[...] + gate2 * acc_ref[...].T
        o_ref[...] = _rms(x2, fw_ref[...])


def _peer(x1, mod3, seq, norm2_w, wq_b, keys_b, u_b, vt_b, final_w):
    t, d = x1.shape
    e = u_b.shape[0]
    qcols = wq_b.shape[1]
    tiles_per_seq = seq // PEER_TM
    n_tb = PEER_TM // LANES
    const2 = lambda i, j: (0, 0)
    return pl.pallas_call(
        _peer_kernel,
        grid=(t // PEER_TM, e // PEER_TE),
        in_specs=[
            pl.BlockSpec((PEER_TM, d), lambda i, j: (i, 0)),
            pl.BlockSpec((1, 1, mod3.shape[2]), lambda i, j: (i // tiles_per_seq, 0, 0)),
            pl.BlockSpec((1, d), const2),
            pl.BlockSpec((d, qcols), const2, pipeline_mode=pl.Buffered(1)),
            pl.BlockSpec(keys_b.shape, lambda i, j: (0, 0, 0), pipeline_mode=pl.Buffered(1)),
            pl.BlockSpec((PEER_TE, d), lambda i, j: (j, 0)),
            pl.BlockSpec((d, PEER_TE), lambda i, j: (0, j)),
            pl.BlockSpec((1, d), const2),
        ],
        out_specs=pl.BlockSpec((PEER_TM, d), lambda i, j: (i, 0)),
        out_shape=jax.ShapeDtypeStruct((t, d), F32),
        scratch_shapes=[
            pltpu.VMEM((d, PEER_TM), BF16),
            pltpu.VMEM((qcols, PEER_TM), BF16),
            pltpu.VMEM((d, PEER_TM), F32),
            pltpu.VMEM((2 * PEER_HEADS, HEAD_DIM, LANES), F32),
            pltpu.VMEM((2 * PEER_HEADS, HEAD_DIM, LANES), F32),
            pltpu.VMEM((2, PEER_TOPK, PEER_HEADS, LANES), F32),
            pltpu.VMEM((PEER_HEADS, n_tb, HEAD_DIM, LANES), F32),
            pltpu.VMEM((PEER_HEADS, n_tb, HEAD_DIM, LANES), F32),
            pltpu.VMEM((PEER_HEADS, n_tb, HEAD_DIM, LANES), BF16),
            pltpu.VMEM((PEER_HEADS, n_tb, HEAD_DIM, LANES), BF16),
            pltpu.VMEM((n_tb, PEER_HEADS, SUBLANES, 2 * SUBLANES, LANES), BF16),
            pltpu.VMEM((n_tb, PEER_HEADS, SUBLANES, 2 * SUBLANES, LANES), BF16),
            pltpu.VMEM((PEER_TE, PEER_TM), F32),
            pltpu.VMEM((PEER_TE, PEER_TM), BF16),
        ],
        compiler_params=pltpu.CompilerParams(
            dimension_semantics=("arbitrary", "arbitrary"), vmem_limit_bytes=VMEM_LIMIT_BYTES),
        name="peer_dense",
    )(x1, mod3, norm2_w, wq_b, keys_b, u_b, vt_b, final_w)


def kernel(x, c, w_ada, b_ada, norm1_w, w_in, hgrn_lb_logits, hgrn_onorm_w, conv_w, conv_onorm_w,
           w_out, norm2_w, peer_w_query, peer_sub_keys, peer_u, peer_v, final_norm_w):
    bsz, seq, d = x.shape
    assert w_ada.shape[0] == 1, "single-layer kernel"
    assert seq % MIX_TS == 0 and seq % PEER_TM == 0 and MIX_TS % CHUNK == 0
    assert peer_u.shape[1] % PEER_TE == 0 and PEER_TE % HEAD_DIM == 0

    c_pad = jnp.pad(c, ((0, SUBLANES - bsz), (0, 0)))
    mod = _modulation(c_pad, w_ada[0], b_ada[0][None, :])
    mod3 = mod[:bsz].reshape(bsz, 1, mod.shape[1])

    u_b, vt_b = _prep_tables(peer_u[0], peer_v[0])

    x1 = _mixer(x, mod3, norm1_w[0][None, :], w_in[0].astype(BF16), hgrn_lb_logits,
                hgrn_onorm_w[0][None, :], conv_w[0], conv_onorm_w[0][None, :], w_out[0].astype(BF16))

    keys_b = peer_sub_keys[0].reshape(2 * PEER_HEADS, HEAD_DIM, HEAD_DIM).astype(BF16)
    out = _peer(x1.reshape(bsz * seq, d), mod3, seq, norm2_w[0][None, :],
                peer_w_query[0].astype(BF16), keys_b, u_b, vt_b, final_norm_w[None, :])
    return out.reshape(bsz, seq, d)
```

```python
import functools
import math

import jax
import jax.numpy as jnp
from jax import lax
from jax.experimental import pallas as pl
from jax.experimental.pallas import tpu as pltpu

F32 = jnp.float32
BF16 = jnp.bfloat16
EPS = 1e-6

LANES = 128
SUBLANES = 8
VMEM_LIMIT_BYTES = 56 * 1024 * 1024

HEAD_DIM = 128
HGRN_HEADS = 4
CONV_GROUPS = 4
PEER_HEADS = 8
PEER_TOPK = 16
LAYER = 0

MOD_TN = 1536
PREP_TE = 512
MIX_TS = 512
CHUNK = 128
DIAG = 8
PEER_TM = 512
PEER_TE = HEAD_DIM * SUBLANES
GATE_BLOCK_A = 4
GATE_BLOCK_B = 4
ACT_CHUNK = 256


def _dot(a, b):
    return jnp.dot(a, b, preferred_element_type=F32)


def _dot_nt(a, b):
    return lax.dot_general(a, b, (((1,), (1,)), ((), ())), preferred_element_type=F32)


def _dot_tn(a, b):
    return lax.dot_general(a, b, (((0,), (0,)), ((), ())), preferred_element_type=F32)


def _split3(x):
    hi = x.astype(BF16)
    r1 = x - hi.astype(F32)
    mid = r1.astype(BF16)
    lo = (r1 - mid.astype(F32)).astype(BF16)
    return hi, mid, lo


def _rms(x, w):
    ms = jnp.mean(x * x, axis=-1, keepdims=True)
    return x * lax.rsqrt(ms + EPS) * w


def _mod_kernel(c_ref, w_ref, b_ref, o_ref):
    c = c_ref[...]
    ca = c * jax.nn.sigmoid(c)
    w = w_ref[...]
    ca_hi = ca.astype(BF16)
    ca_lo = (ca - ca_hi.astype(F32)).astype(BF16)
    w_hi = w.astype(BF16)
    w_lo = (w - w_hi.astype(F32)).astype(BF16)
    acc = _dot(ca_hi, w_hi) + _dot(ca_lo, w_hi) + _dot(ca_hi, w_lo)
    o_ref[...] = acc + b_ref[...]


def _modulation(c_pad, w_ada, b_ada):
    rows, d = c_pad.shape
    n = w_ada.shape[1]
    return pl.pallas_call(
        _mod_kernel,
        grid=(n // MOD_TN,),
        in_specs=[
            pl.BlockSpec((rows, d), lambda j: (0, 0)),
            pl.BlockSpec((d, MOD_TN), lambda j: (0, j)),
            pl.BlockSpec((1, MOD_TN), lambda j: (0, j)),
        ],
        out_specs=pl.BlockSpec((rows, MOD_TN), lambda j: (0, j)),
        out_shape=jax.ShapeDtypeStruct((rows, n), F32),
        compiler_params=pltpu.CompilerParams(
            dimension_semantics=("arbitrary",), vmem_limit_bytes=VMEM_LIMIT_BYTES),
        name="adaln_mod",
    )(c_pad, w_ada, b_ada)


def _prep_kernel(u_ref, v_ref, ub_ref, vt_ref):
    ub_ref[...] = u_ref[...].astype(BF16)
    vt_ref[...] = v_ref[...].T.astype(BF16)


def _prep_tables(peer_u, peer_v):
    e, d = peer_u.shape
    return pl.pallas_call(
        _prep_kernel,
        grid=(e // PREP_TE,),
        in_specs=[
            pl.BlockSpec((PREP_TE, d), lambda j: (j, 0)),
            pl.BlockSpec((PREP_TE, d), lambda j: (j, 0)),
        ],
        out_specs=[
            pl.BlockSpec((PREP_TE, d), lambda j: (j, 0)),
            pl.BlockSpec((d, PREP_TE), lambda j: (0, j)),
        ],
        out_shape=[
            jax.ShapeDtypeStruct((e, d), BF16),
            jax.ShapeDtypeStruct((d, e), BF16),
        ],
        compiler_params=pltpu.CompilerParams(
            dimension_semantics=("arbitrary",), vmem_limit_bytes=VMEM_LIMIT_BYTES),
        name="peer_tables",
    )(peer_u, peer_v)


def _mixer_kernel(x_ref, mod_ref, n1w_ref, win_ref, lbl_ref, honw_ref, cw_ref, conw_ref, wout_ref,
                  o_ref, proj_ref, st_ref, ubuf_ref, ycat_ref, kk_ref, bb_ref, sc_ref):
    ts, d = x_ref.shape[1], x_ref.shape[2]
    hw = HGRN_HEADS * HEAD_DIM
    cwid = CONV_GROUPS * HEAD_DIM
    n_chunks = ts // CHUNK

    @pl.when(pl.program_id(1) == 0)
    def _():
        st_ref[...] = jnp.zeros_like(st_ref)
        ubuf_ref[0:SUBLANES, :] = jnp.zeros((SUBLANES, cwid), F32)

    x = x_ref[0]
    mod = mod_ref[0]
    shift1 = mod[:, 0:d]
    scale1 = mod[:, d:2 * d]
    gate1 = mod[:, 2 * d:3 * d]
    h = _rms(x, n1w_ref[...]) * (1.0 + scale1) + shift1
    proj_ref[...] = _dot(h.astype(BF16), win_ref[...])

    lbl = lbl_ref[...]
    lbe = jnp.exp(lbl - jnp.max(lbl, axis=0, keepdims=True))
    lbp = lbe / jnp.sum(lbe, axis=0, keepdims=True)
    lb = jnp.sum(lbp[0:LAYER + 1, :], axis=0, keepdims=True)

    row = lax.broadcasted_iota(jnp.int32, (CHUNK, CHUNK), 0)
    col = lax.broadcasted_iota(jnp.int32, (CHUNK, CHUNK), 1)
    tril = jnp.where(col <= row, 1.0, 0.0).astype(BF16)
    rowd = lax.broadcasted_iota(jnp.int32, (CHUNK, HEAD_DIM), 0)
    sub8 = lax.broadcasted_iota(jnp.int32, (DIAG, HEAD_DIM), 0)
    lane8 = lax.broadcasted_iota(jnp.int32, (DIAG, CHUNK), 1)
    halves = []
    hs = CHUNK // 2
    while hs >= DIAG:
        halves.append(hs)
        hs //= 2

    def chunk_body(c, carry):
        r0 = pl.multiple_of(c * CHUNK, CHUNK)
        fraw = proj_ref[pl.ds(r0, CHUNK), hw:2 * hw]
        f = lb + (1.0 - lb) * jax.nn.sigmoid(fraw)
        logf = jnp.log(f)
        kk = (1.0 - lb) * jax.nn.sigmoid(-fraw)
        hi, mid, lo = _split3(logf)
        b_all = _dot(tril, hi) + _dot(tril, mid) + _dot(tril, lo)
        kk_ref[...] = kk
        bb_ref[...] = b_all

        def diag_body(j, carry2):
            for half in range(2):
                diag_block(pl.multiple_of((2 * j + half) * DIAG, DIAG))
            return carry2

        def diag_block(rr):
            for hd in range(HGRN_HEADS):
                sl = slice(hd * HEAD_DIM, (hd + 1) * HEAD_DIM)
                qb = proj_ref[pl.ds(r0 + rr, DIAG), sl]
                kb = kk_ref[pl.ds(rr, DIAG), sl]
                bb = bb_ref[pl.ds(rr, DIAG), sl]
                slab = jnp.zeros((DIAG, CHUNK), F32)
                for s in range(DIAG):
                    arg = jnp.where(sub8 >= s, bb - bb[s:s + 1, :], -1e30)
                    p = qb * kb[s:s + 1, :] * jnp.exp(arg)
                    colsum = jnp.sum(p, axis=-1, keepdims=True)
                    slab = jnp.where(lane8 == rr + s, colsum, slab)
                sc_ref[hd, pl.ds(rr, DIAG), :] = slab

        lax.fori_loop(0, CHUNK // (2 * DIAG), diag_body, 0)

        for hd in range(HGRN_HEADS):
            sl = slice(hd * HEAD_DIM, (hd + 1) * HEAD_DIM)
            q = proj_ref[pl.ds(r0, CHUNK), sl]
            v = proj_ref[pl.ds(r0, CHUNK), 2 * hw + hd * HEAD_DIM:2 * hw + (hd + 1) * HEAD_DIM]
            gout = proj_ref[pl.ds(r0, CHUNK), 3 * hw + hd * HEAD_DIM:3 * hw + (hd + 1) * HEAD_DIM]
            k = kk[:, sl]
            b = b_all[:, sl]
            scores = sc_ref[hd]
            for hs in halves:
                parts = []
                for blk in range(CHUNK // (2 * hs)):
                    r = blk * 2 * hs + hs - 1
                    parts.append(jnp.broadcast_to(b[r:r + 1, :], (2 * hs, HEAD_DIM)))
                bref = parts[0] if len(parts) == 1 else jnp.concatenate(parts, axis=0)
                dd = b - bref
                second = ((rowd // hs) % 2) == 1
                e = jnp.exp(jnp.where(second, dd, -dd))
                ql = jnp.where(second, q * e, 0.0)
                kl = jnp.where(second, 0.0, k * e)
                sl_scores = _dot_nt(ql.astype(BF16), kl.astype(BF16))
                valid = ((row // (2 * hs)) == (col // (2 * hs))) & (((row // hs) % 2) == 1) & (((col // hs) % 2) == 0)
                scores = jnp.where(valid, sl_scores, scores)
            st = st_ref[hd]
            o = _dot(scores.astype(BF16), v.astype(BF16))
            o = o + _dot_nt((q * jnp.exp(b)).astype(BF16), st.astype(BF16))
            b_end = b[CHUNK - 1:CHUNK, :]
            ks = k * jnp.exp(b_end - b)
            st_ref[hd] = st * jnp.exp(b_end) + _dot_tn(v.astype(BF16), ks.astype(BF16))
            oh = _rms(o, honw_ref[:, sl]) * (gout * jax.nn.sigmoid(gout))
            ycat_ref[pl.ds(r0, CHUNK), sl] = oh
        return carry

    lax.fori_loop(0, n_chunks, chunk_body, 0)

    bg = proj_ref[:, 4 * hw:4 * hw + cwid]
    cg = proj_ref[:, 4 * hw + cwid:4 * hw + 2 * cwid]
    xin = proj_ref[:, 4 * hw + 2 * cwid:4 * hw + 3 * cwid]
    u = cg * xin
    ubuf_ref[SUBLANES:SUBLANES + ts, :] = u
    cw = cw_ref[...]
    y = (cw[0:1, :] * ubuf_ref[SUBLANES - 2:SUBLANES - 2 + ts, :]
         + cw[1:2, :] * ubuf_ref[SUBLANES - 1:SUBLANES - 1 + ts, :]
         + cw[2:3, :] * u)
    ubuf_ref[0:SUBLANES, :] = u[ts - SUBLANES:ts, :]
    y = bg * y
    for g in range(CONV_GROUPS):
        sl = slice(g * HEAD_DIM, (g + 1) * HEAD_DIM)
        ycat_ref[:, hw + g * HEAD_DIM:hw + (g + 1) * HEAD_DIM] = _rms(y[:, sl], conw_ref[:, sl])

    mix = _dot(ycat_ref[...].astype(BF16), wout_ref[...])
    o_ref[0] = x + gate1 * mix


def _mixer(x, mod3, norm1_w, w_in_b, lb_logits, honw, conv_w, conw, w_out_b):
    bsz, seq, d = x.shape
    ncols = w_in_b.shape[1]
    hw = HGRN_HEADS * HEAD_DIM
    cwid = CONV_GROUPS * HEAD_DIM
    const2 = lambda b, s: (0, 0)
    return pl.pallas_call(
        _mixer_kernel,
        grid=(bsz, seq // MIX_TS),
        in_specs=[
            pl.BlockSpec((1, MIX_TS, d), lambda b, s: (b, s, 0)),
            pl.BlockSpec((1, 1, mod3.shape[2]), lambda b, s: (b, 0, 0)),
            pl.BlockSpec((1, d), const2),
            pl.BlockSpec((d, ncols), const2, pipeline_mode=pl.Buffered(1)),
            pl.BlockSpec(lb_logits.shape, const2),
            pl.BlockSpec((1, hw), const2),
            pl.BlockSpec(conv_w.shape, const2),
            pl.BlockSpec((1, cwid), const2),
            pl.BlockSpec((hw + cwid, d), const2, pipeline_mode=pl.Buffered(1)),
        ],
        out_specs=pl.BlockSpec((1, MIX_TS, d), lambda b, s: (b, s, 0)),
        out_shape=jax.ShapeDtypeStruct((bsz, seq, d), F32),
        scratch_shapes=[
            pltpu.VMEM((MIX_TS, ncols), F32),
            pltpu.VMEM((HGRN_HEADS, HEAD_DIM, HEAD_DIM), F32),
            pltpu.VMEM((SUBLANES + MIX_TS, cwid), F32),
            pltpu.VMEM((MIX_TS, hw + cwid), F32),
            pltpu.VMEM((CHUNK, hw), F32),
            pltpu.VMEM((CHUNK, hw), F32),
            pltpu.VMEM((HGRN_HEADS, CHUNK, CHUNK), F32),
        ],
        compiler_params=pltpu.CompilerParams(
            dimension_semantics=("arbitrary", "arbitrary"), vmem_limit_bytes=VMEM_LIMIT_BYTES),
        name="token_mixer",
    )(x, mod3, norm1_w, w_in_b, lb_logits, honw, conv_w, conw, w_out_b)


def _candidates():
    return [(i, j) for i in range(PEER_TOPK) for j in range(PEER_TOPK)
            if (i + 1) * (j + 1) <= PEER_TOPK]


def _peer_kernel(x_ref, mod_ref, n2w_ref, wq_ref, keys_ref, u_ref, vt_ref, fw_ref,
                 o_ref, h2t_ref, qt_ref, acc_ref, st_ref, rk_ref, vals_ref,
                 n1f_ref, e1f_ref, r2b_ref, e2b_ref, bcn_ref, bce_ref, p_ref):
    tm, d = x_ref.shape
    j = pl.program_id(1)
    n_tb = tm // LANES
    n_sub = HEAD_DIM
    mod = mod_ref[0]

    @pl.when(j == 0)
    def _route():
        shift2 = mod[:, 3 * d:4 * d]
        scale2 = mod[:, 4 * d:5 * d]
        h2 = _rms(x_ref[...], n2w_ref[...]) * (1.0 + scale2) + shift2
        h2b = h2.astype(BF16)
        h2t_ref[...] = h2.T.astype(BF16)
        qt_ref[...] = _dot(h2b, wq_ref[...]).T.astype(BF16)
        acc_ref[...] = jnp.zeros_like(acc_ref)

        n_iota = lax.broadcasted_iota(jnp.int32, (n_sub, LANES), 0).astype(F32)
        head_iota = lax.broadcasted_iota(jnp.int32, (PEER_HEADS, LANES), 0)
        cands = _candidates()

        def tb_body(tb, carry):
            lanes = pl.ds(pl.multiple_of(tb * LANES, LANES), LANES)

            def extract(hp, hh, pp, break_ties):
                s = st_ref[hp]
                rk_ref[hp] = jnp.full((n_sub, LANES), float(PEER_TOPK), F32)
                for i in range(PEER_TOPK):
                    m = jnp.max(s, axis=0, keepdims=True)
                    if break_ties:
                        first = jnp.min(jnp.where(s == m, n_iota, float(n_sub)), axis=0, keepdims=True)
                        hit = n_iota == first
                    else:
                        hit = s == m
                    rk_ref[hp] = jnp.where(hit, float(i), rk_ref[hp])
                    s = jnp.where(hit, -jnp.inf, s)
                    vals_ref[pp, i] = jnp.where(head_iota == hh, m, vals_ref[pp, i])
                picked = jnp.sum(jnp.where(rk_ref[hp] < float(PEER_TOPK), 1.0, 0.0), axis=0, keepdims=True)
                return jnp.max(jnp.abs(picked - float(PEER_TOPK)))

            def hh_body(h2i, carry2):
                hps = [(2 * (2 * h2i + dh) + pp, 2 * h2i + dh, pp) for dh in range(2) for pp in range(2)]
                for hp, _, _ in hps:
                    qrows = pl.ds(pl.multiple_of(hp * HEAD_DIM, HEAD_DIM), HEAD_DIM)
                    st_ref[hp] = _dot(keys_ref[hp], qt_ref[qrows, lanes])
                excess = [extract(hp, hh, pp, False) for hp, hh, pp in hps]
                tied = functools.reduce(jnp.maximum, excess) > 0.0

                @pl.when(tied)
                def _():
                    for hp, hh, pp in hps:
                        extract(hp, hh, pp, True)

                return carry2

            lax.fori_loop(0, PEER_HEADS // 2, hh_body, 0)

            v1 = [vals_ref[0, i] for i in range(PEER_TOPK)]
            v2 = [vals_ref[1, i] for i in range(PEER_TOPK)]
            c = [v1[i] + v2[jj] for (i, jj) in cands]
            c00 = c[0]
            cnt = [jnp.zeros((PEER_HEADS, LANES), F32) for _ in range(PEER_TOPK)]
            z = jnp.zeros((PEER_HEADS, LANES), F32)
            big = float(PEER_TOPK * PEER_TOPK)
            for _ in range(PEER_TOPK):
                m = functools.reduce(jnp.maximum, c)
                first = functools.reduce(
                    jnp.minimum,
                    [jnp.where(c[k] == m, float(i * PEER_TOPK + jj), big) for k, (i, jj) in enumerate(cands)])
                z = z + jnp.exp(m - c00)
                for k, (i, jj) in enumerate(cands):
                    hit = first == float(i * PEER_TOPK + jj)
                    cnt[i] = cnt[i] + jnp.where(hit, 1.0, 0.0)
                    c[k] = jnp.where(hit, -jnp.inf, c[k])
            zinv = 1.0 / z

            for hh in range(PEER_HEADS):
                rank1 = rk_ref[2 * hh]
                n1 = jnp.zeros((n_sub, LANES), F32)
                for i in range(PEER_TOPK):
                    n1 = jnp.where(rank1 == float(i), cnt[i][hh:hh + 1, :], n1)
                n1f_ref[hh, tb] = n1
                e1f_ref[hh, tb] = jnp.exp(st_ref[2 * hh] - v1[0][hh:hh + 1, :])
                r2b_ref[hh, tb] = rk_ref[2 * hh + 1].astype(BF16)
                e2 = jnp.exp(st_ref[2 * hh + 1] - v2[0][hh:hh + 1, :]) * zinv[hh:hh + 1, :]
                e2b_ref[hh, tb] = e2.astype(BF16)
            return carry

        lax.fori_loop(0, n_tb, tb_body, 0)

    n_a = PEER_TE // n_sub
    assert n_a == SUBLANES
    arows = pl.ds(pl.multiple_of(j * n_a, SUBLANES), SUBLANES)
    pack = 2 * SUBLANES
    zero_b = jnp.zeros((pack, LANES), BF16)
    n_bv = n_sub // pack
    assert n_a % GATE_BLOCK_A == 0 and n_bv % GATE_BLOCK_B == 0

    for tb in range(n_tb):
        lanes = slice(tb * LANES, (tb + 1) * LANES)
        for hh in range(PEER_HEADS):
            n1t = n1f_ref[hh, tb, arows, :]
            e1t = e1f_ref[hh, tb, arows, :]
            for al in range(n_a):
                bcn_ref[tb, hh, al] = jnp.broadcast_to(n1t[al:al + 1, :], (pack, LANES)).astype(BF16)
                bce_ref[tb, hh, al] = jnp.broadcast_to(e1t[al:al + 1, :], (pack, LANES)).astype(BF16)

    nblk_a = n_a // GATE_BLOCK_A
    nblk_b = n_bv // GATE_BLOCK_B

    def gate_block(i, carry):
        tb = i // (nblk_a * nblk_b)
        a0 = ((i // nblk_b) % nblk_a) * GATE_BLOCK_A
        b0 = (i % nblk_b) * GATE_BLOCK_B
        lanes = pl.ds(pl.multiple_of(tb * LANES, LANES), LANES)
        w = [[zero_b for _ in range(GATE_BLOCK_B)] for _ in range(GATE_BLOCK_A)]
        for hh in range(PEER_HEADS):
            n1 = [bcn_ref[tb, hh, a0 + ia] for ia in range(GATE_BLOCK_A)]
            e1 = [bce_ref[tb, hh, a0 + ia] for ia in range(GATE_BLOCK_A)]
            for ib in range(GATE_BLOCK_B):
                rows = pl.ds(pl.multiple_of((b0 + ib) * pack, pack), pack)
                r2 = r2b_ref[hh, tb, rows, :]
                e2 = e2b_ref[hh, tb, rows, :]
                for ia in range(GATE_BLOCK_A):
                    w[ia][ib] = w[ia][ib] + jnp.where(r2 < n1[ia], e2, zero_b) * e1[ia]
        for ia in range(GATE_BLOCK_A):
            for ib in range(GATE_BLOCK_B):
                erows = pl.ds(pl.multiple_of((a0 + ia) * n_sub + (b0 + ib) * pack, pack), pack)
                p_ref[erows, lanes] = w[ia][ib]
        return carry

    lax.fori_loop(0, n_tb * nblk_a * nblk_b, gate_block, 0)

    for q in range(PEER_TE // ACT_CHUNK):
        rq = slice(q * ACT_CHUNK, (q + 1) * ACT_CHUNK)
        xa = _dot(u_ref[rq, :], h2t_ref[...])
        g = 0.5 * xa * (1.0 + lax.erf(xa * math.sqrt(0.5)))
        p_ref[rq, :] = g.astype(BF16) * p_ref[rq, :]
    acc_ref[...] += _dot(vt_ref[...], p_ref[...])

    @pl.when(j == pl.num_programs(1) - 1)
    def _finish():
        gate2 = mod[:, 5 * d:6 * d]
        x2 = x_ref[...] + gate2 * acc_ref[...].T
        o_ref[...] = _rms(x2, fw_ref[...])


def _peer(x1, mod3, seq, norm2_w, wq_b, keys_b, u_b, vt_b, final_w):
    t, d = x1.shape
    e = u_b.shape[0]
    qcols = wq_b.shape[1]
    tiles_per_seq = seq // PEER_TM
    n_tb = PEER_TM // LANES
    const2 = lambda i, j: (0, 0)
    return pl.pallas_call(
        _peer_kernel,
        grid=(t // PEER_TM, e // PEER_TE),
        in_specs=[
            pl.BlockSpec((PEER_TM, d), lambda i, j: (i, 0)),
            pl.BlockSpec((1, 1, mod3.shape[2]), lambda i, j: (i // tiles_per_seq, 0, 0)),
            pl.BlockSpec((1, d), const2),
            pl.BlockSpec((d, qcols), const2, pipeline_mode=pl.Buffered(1)),
            pl.BlockSpec(keys_b.shape, lambda i, j: (0, 0, 0), pipeline_mode=pl.Buffered(1)),
            pl.BlockSpec((PEER_TE, d), lambda i, j: (j, 0)),
            pl.BlockSpec((d, PEER_TE), lambda i, j: (0, j)),
            pl.BlockSpec((1, d), const2),
        ],
        out_specs=pl.BlockSpec((PEER_TM, d), lambda i, j: (i, 0)),
        out_shape=jax.ShapeDtypeStruct((t, d), F32),
        scratch_shapes=[
            pltpu.VMEM((d, PEER_TM), BF16),
            pltpu.VMEM((qcols, PEER_TM), BF16),
            pltpu.VMEM((d, PEER_TM), F32),
            pltpu.VMEM((2 * PEER_HEADS, HEAD_DIM, LANES), F32),
            pltpu.VMEM((2 * PEER_HEADS, HEAD_DIM, LANES), F32),
            pltpu.VMEM((2, PEER_TOPK, PEER_HEADS, LANES), F32),
            pltpu.VMEM((PEER_HEADS, n_tb, HEAD_DIM, LANES), F32),
            pltpu.VMEM((PEER_HEADS, n_tb, HEAD_DIM, LANES), F32),
            pltpu.VMEM((PEER_HEADS, n_tb, HEAD_DIM, LANES), BF16),
            pltpu.VMEM((PEER_HEADS, n_tb, HEAD_DIM, LANES), BF16),
            pltpu.VMEM((n_tb, PEER_HEADS, SUBLANES, 2 * SUBLANES, LANES), BF16),
            pltpu.VMEM((n_tb, PEER_HEADS, SUBLANES, 2 * SUBLANES, LANES), BF16),
            pltpu.VMEM((PEER_TE, PEER_TM), BF16),
        ],
        compiler_params=pltpu.CompilerParams(
            dimension_semantics=("arbitrary", "arbitrary"), vmem_limit_bytes=VMEM_LIMIT_BYTES),
        name="peer_dense",
    )(x1, mod3, norm2_w, wq_b, keys_b, u_b, vt_b, final_w)


def kernel(x, c, w_ada, b_ada, norm1_w, w_in, hgrn_lb_logits, hgrn_onorm_w, conv_w, conv_onorm_w,
           w_out, norm2_w, peer_w_query, peer_sub_keys, peer_u, peer_v, final_norm_w):
    bsz, seq, d = x.shape
    assert w_ada.shape[0] == 1, "single-layer kernel"
    assert seq % MIX_TS == 0 and seq % PEER_TM == 0 and MIX_TS % CHUNK == 0
    assert peer_u.shape[1] % PEER_TE == 0 and PEER_TE % HEAD_DIM == 0

    c_pad = jnp.pad(c, ((0, SUBLANES - bsz), (0, 0)))
    mod = _modulation(c_pad, w_ada[0], b_ada[0][None, :])
    mod3 = mod[:bsz].reshape(bsz, 1, mod.shape[1])

    u_b, vt_b = _prep_tables(peer_u[0], peer_v[0])

    x1 = _mixer(x, mod3, norm1_w[0][None, :], w_in[0].astype(BF16), hgrn_lb_logits,
                hgrn_onorm_w[0][None, :], conv_w[0], conv_onorm_w[0][None, :], w_out[0].astype(BF16))

    keys_b = peer_sub_keys[0].reshape(2 * PEER_HEADS, HEAD_DIM, HEAD_DIM).astype(BF16)
    out = _peer(x1.reshape(bsz * seq, d), mod3, seq, norm2_w[0][None, :],
                peer_w_query[0].astype(BF16), keys_b, u_b, vt_b, final_norm_w[None, :])
    return out.reshape(bsz, seq, d)
```

```python
import functools
import math

import jax
import jax.numpy as jnp
from jax import lax
from jax.experimental import pallas as pl
from jax.experimental.pallas import tpu as pltpu

F32 = jnp.float32
BF16 = jnp.bfloat16
EPS = 1e-6

LANES = 128
SUBLANES = 8
VMEM_LIMIT_BYTES = 56 * 1024 * 1024

HEAD_DIM = 128
HGRN_HEADS = 4
CONV_GROUPS = 4
PEER_HEADS = 8
PEER_TOPK = 16
LAYER = 0

MOD_TN = 1536
PREP_TE = 512
MIX_TS = 512
CHUNK = 128
DIAG = 8
PEER_TM = 512
PEER_TE = HEAD_DIM * SUBLANES
GATE_BLOCK_A = 4
GATE_BLOCK_B = 4
ACT_CHUNK = 128


def _dot(a, b):
    return jnp.dot(a, b, preferred_element_type=F32)


def _dot_nt(a, b):
    return lax.dot_general(a, b, (((1,), (1,)), ((), ())), preferred_element_type=F32)


def _dot_tn(a, b):
    return lax.dot_general(a, b, (((0,), (0,)), ((), ())), preferred_element_type=F32)


def _split3(x):
    hi = x.astype(BF16)
    r1 = x - hi.astype(F32)
    mid = r1.astype(BF16)
    lo = (r1 - mid.astype(F32)).astype(BF16)
    return hi, mid, lo


def _rms(x, w):
    ms = jnp.mean(x * x, axis=-1, keepdims=True)
    return x * lax.rsqrt(ms + EPS) * w


def _mod_kernel(c_ref, w_ref, b_ref, o_ref):
    c = c_ref[...]
    ca = c * jax.nn.sigmoid(c)
    w = w_ref[...]
    ca_hi = ca.astype(BF16)
    ca_lo = (ca - ca_hi.astype(F32)).astype(BF16)
    w_hi = w.astype(BF16)
    w_lo = (w - w_hi.astype(F32)).astype(BF16)
    acc = _dot(ca_hi, w_hi) + _dot(ca_lo, w_hi) + _dot(ca_hi, w_lo)
    o_ref[...] = acc + b_ref[...]


def _modulation(c_pad, w_ada, b_ada):
    rows, d = c_pad.shape
    n = w_ada.shape[1]
    return pl.pallas_call(
        _mod_kernel,
        grid=(n // MOD_TN,),
        in_specs=[
            pl.BlockSpec((rows, d), lambda j: (0, 0)),
            pl.BlockSpec((d, MOD_TN), lambda j: (0, j)),
            pl.BlockSpec((1, MOD_TN), lambda j: (0, j)),
        ],
        out_specs=pl.BlockSpec((rows, MOD_TN), lambda j: (0, j)),
        out_shape=jax.ShapeDtypeStruct((rows, n), F32),
        compiler_params=pltpu.CompilerParams(
            dimension_semantics=("arbitrary",), vmem_limit_bytes=VMEM_LIMIT_BYTES),
        name="adaln_mod",
    )(c_pad, w_ada, b_ada)


def _prep_kernel(u_ref, v_ref, ub_ref, vt_ref):
    ub_ref[...] = u_ref[...].astype(BF16)
    vt_ref[...] = v_ref[...].T.astype(BF16)


def _prep_tables(peer_u, peer_v):
    e, d = peer_u.shape
    return pl.pallas_call(
        _prep_kernel,
        grid=(e // PREP_TE,),
        in_specs=[
            pl.BlockSpec((PREP_TE, d), lambda j: (j, 0)),
            pl.BlockSpec((PREP_TE, d), lambda j: (j, 0)),
        ],
        out_specs=[
            pl.BlockSpec((PREP_TE, d), lambda j: (j, 0)),
            pl.BlockSpec((d, PREP_TE), lambda j: (0, j)),
        ],
        out_shape=[
            jax.ShapeDtypeStruct((e, d), BF16),
            jax.ShapeDtypeStruct((d, e), BF16),
        ],
        compiler_params=pltpu.CompilerParams(
            dimension_semantics=("arbitrary",), vmem_limit_bytes=VMEM_LIMIT_BYTES),
        name="peer_tables",
    )(peer_u, peer_v)


def _mixer_kernel(x_ref, mod_ref, n1w_ref, win_ref, lbl_ref, honw_ref, cw_ref, conw_ref, wout_ref,
                  o_ref, proj_ref, st_ref, ubuf_ref, ycat_ref, kk_ref, bb_ref, sc_ref):
    ts, d = x_ref.shape[1], x_ref.shape[2]
    hw = HGRN_HEADS * HEAD_DIM
    cwid = CONV_GROUPS * HEAD_DIM
    n_chunks = ts // CHUNK

    @pl.when(pl.program_id(1) == 0)
    def _():
        st_ref[...] = jnp.zeros_like(st_ref)
        ubuf_ref[0:SUBLANES, :] = jnp.zeros((SUBLANES, cwid), F32)

    x = x_ref[0]
    mod = mod_ref[0]
    shift1 = mod[:, 0:d]
    scale1 = mod[:, d:2 * d]
    gate1 = mod[:, 2 * d:3 * d]
    h = _rms(x, n1w_ref[...]) * (1.0 + scale1) + shift1
    proj_ref[...] = _dot(h.astype(BF16), win_ref[...])

    lbl = lbl_ref[...]
    lbe = jnp.exp(lbl - jnp.max(lbl, axis=0, keepdims=True))
    lbp = lbe / jnp.sum(lbe, axis=0, keepdims=True)
    lb = jnp.sum(lbp[0:LAYER + 1, :], axis=0, keepdims=True)

    row = lax.broadcasted_iota(jnp.int32, (CHUNK, CHUNK), 0)
    col = lax.broadcasted_iota(jnp.int32, (CHUNK, CHUNK), 1)
    tril = jnp.where(col <= row, 1.0, 0.0).astype(BF16)
    rowd = lax.broadcasted_iota(jnp.int32, (CHUNK, HEAD_DIM), 0)
    sub8 = lax.broadcasted_iota(jnp.int32, (DIAG, HEAD_DIM), 0)
    lane8 = lax.broadcasted_iota(jnp.int32, (DIAG, CHUNK), 1)
    halves = []
    hs = CHUNK // 2
    while hs >= DIAG:
        halves.append(hs)
        hs //= 2

    def chunk_body(c, carry):
        r0 = pl.multiple_of(c * CHUNK, CHUNK)
        fraw = proj_ref[pl.ds(r0, CHUNK), hw:2 * hw]
        f = lb + (1.0 - lb) * jax.nn.sigmoid(fraw)
        logf = jnp.log(f)
        kk = (1.0 - lb) * jax.nn.sigmoid(-fraw)
        hi, mid, lo = _split3(logf)
        b_all = _dot(tril, hi) + _dot(tril, mid) + _dot(tril, lo)
        kk_ref[...] = kk
        bb_ref[...] = b_all

        def diag_body(j, carry2):
            for half in range(2):
                diag_block(pl.multiple_of((2 * j + half) * DIAG, DIAG))
            return carry2

        def diag_block(rr):
            for hd in range(HGRN_HEADS):
                sl = slice(hd * HEAD_DIM, (hd + 1) * HEAD_DIM)
                qb = proj_ref[pl.ds(r0 + rr, DIAG), sl]
                kb = kk_ref[pl.ds(rr, DIAG), sl]
                bb = bb_ref[pl.ds(rr, DIAG), sl]
                slab = jnp.zeros((DIAG, CHUNK), F32)
                for s in range(DIAG):
                    arg = jnp.where(sub8 >= s, bb - bb[s:s + 1, :], -1e30)
                    p = qb * kb[s:s + 1, :] * jnp.exp(arg)
                    colsum = jnp.sum(p, axis=-1, keepdims=True)
                    slab = jnp.where(lane8 == rr + s, colsum, slab)
                sc_ref[hd, pl.ds(rr, DIAG), :] = slab

        lax.fori_loop(0, CHUNK // (2 * DIAG), diag_body, 0)

        for hd in range(HGRN_HEADS):
            sl = slice(hd * HEAD_DIM, (hd + 1) * HEAD_DIM)
            q = proj_ref[pl.ds(r0, CHUNK), sl]
            v = proj_ref[pl.ds(r0, CHUNK), 2 * hw + hd * HEAD_DIM:2 * hw + (hd + 1) * HEAD_DIM]
            gout = proj_ref[pl.ds(r0, CHUNK), 3 * hw + hd * HEAD_DIM:3 * hw + (hd + 1) * HEAD_DIM]
            k = kk[:, sl]
            b = b_all[:, sl]
            scores = sc_ref[hd]
            for hs in halves:
                parts = []
                for blk in range(CHUNK // (2 * hs)):
                    r = blk * 2 * hs + hs - 1
                    parts.append(jnp.broadcast_to(b[r:r + 1, :], (2 * hs, HEAD_DIM)))
                bref = parts[0] if len(parts) == 1 else jnp.concatenate(parts, axis=0)
                dd = b - bref
                second = ((rowd // hs) % 2) == 1
                e = jnp.exp(jnp.where(second, dd, -dd))
                ql = jnp.where(second, q * e, 0.0)
                kl = jnp.where(second, 0.0, k * e)
                sl_scores = _dot_nt(ql.astype(BF16), kl.astype(BF16))
                valid = ((row // (2 * hs)) == (col // (2 * hs))) & (((row // hs) % 2) == 1) & (((col // hs) % 2) == 0)
                scores = jnp.where(valid, sl_scores, scores)
            st = st_ref[hd]
            o = _dot(scores.astype(BF16), v.astype(BF16))
            o = o + _dot_nt((q * jnp.exp(b)).astype(BF16), st.astype(BF16))
            b_end = b[CHUNK - 1:CHUNK, :]
            ks = k * jnp.exp(b_end - b)
            st_ref[hd] = st * jnp.exp(b_end) + _dot_tn(v.astype(BF16), ks.astype(BF16))
            oh = _rms(o, honw_ref[:, sl]) * (gout * jax.nn.sigmoid(gout))
            ycat_ref[pl.ds(r0, CHUNK), sl] = oh
        return carry

    lax.fori_loop(0, n_chunks, chunk_body, 0)

    bg = proj_ref[:, 4 * hw:4 * hw + cwid]
    cg = proj_ref[:, 4 * hw + cwid:4 * hw + 2 * cwid]
    xin = proj_ref[:, 4 * hw + 2 * cwid:4 * hw + 3 * cwid]
    u = cg * xin
    ubuf_ref[SUBLANES:SUBLANES + ts, :] = u
    cw = cw_ref[...]
    y = (cw[0:1, :] * ubuf_ref[SUBLANES - 2:SUBLANES - 2 + ts, :]
         + cw[1:2, :] * ubuf_ref[SUBLANES - 1:SUBLANES - 1 + ts, :]
         + cw[2:3, :] * u)
    ubuf_ref[0:SUBLANES, :] = u[ts - SUBLANES:ts, :]
    y = bg * y
    for g in range(CONV_GROUPS):
        sl = slice(g * HEAD_DIM, (g + 1) * HEAD_DIM)
        ycat_ref[:, hw + g * HEAD_DIM:hw + (g + 1) * HEAD_DIM] = _rms(y[:, sl], conw_ref[:, sl])

    mix = _dot(ycat_ref[...].astype(BF16), wout_ref[...])
    o_ref[0] = x + gate1 * mix


def _mixer(x, mod3, norm1_w, w_in_b, lb_logits, honw, conv_w, conw, w_out_b):
    bsz, seq, d = x.shape
    ncols = w_in_b.shape[1]
    hw = HGRN_HEADS * HEAD_DIM
    cwid = CONV_GROUPS * HEAD_DIM
    const2 = lambda b, s: (0, 0)
    return pl.pallas_call(
        _mixer_kernel,
        grid=(bsz, seq // MIX_TS),
        in_specs=[
            pl.BlockSpec((1, MIX_TS, d), lambda b, s: (b, s, 0)),
            pl.BlockSpec((1, 1, mod3.shape[2]), lambda b, s: (b, 0, 0)),
            pl.BlockSpec((1, d), const2),
            pl.BlockSpec((d, ncols), const2, pipeline_mode=pl.Buffered(1)),
            pl.BlockSpec(lb_logits.shape, const2),
            pl.BlockSpec((1, hw), const2),
            pl.BlockSpec(conv_w.shape, const2),
            pl.BlockSpec((1, cwid), const2),
            pl.BlockSpec((hw + cwid, d), const2, pipeline_mode=pl.Buffered(1)),
        ],
        out_specs=pl.BlockSpec((1, MIX_TS, d), lambda b, s: (b, s, 0)),
        out_shape=jax.ShapeDtypeStruct((bsz, seq, d), F32),
        scratch_shapes=[
            pltpu.VMEM((MIX_TS, ncols), F32),
            pltpu.VMEM((HGRN_HEADS, HEAD_DIM, HEAD_DIM), F32),
            pltpu.VMEM((SUBLANES + MIX_TS, cwid), F32),
            pltpu.VMEM((MIX_TS, hw + cwid), F32),
            pltpu.VMEM((CHUNK, hw), F32),
            pltpu.VMEM((CHUNK, hw), F32),
            pltpu.VMEM((HGRN_HEADS, CHUNK, CHUNK), F32),
        ],
        compiler_params=pltpu.CompilerParams(
            dimension_semantics=("arbitrary", "arbitrary"), vmem_limit_bytes=VMEM_LIMIT_BYTES),
        name="token_mixer",
    )(x, mod3, norm1_w, w_in_b, lb_logits, honw, conv_w, conw, w_out_b)


def _candidates():
    return [(i, j) for i in range(PEER_TOPK) for j in range(PEER_TOPK)
            if (i + 1) * (j + 1) <= PEER_TOPK]


def _peer_kernel(x_ref, mod_ref, n2w_ref, wq_ref, keys_ref, u_ref, vt_ref, fw_ref,
                 o_ref, h2t_ref, qt_ref, acc_ref, st_ref, rk_ref, vals_ref,
                 n1f_ref, e1f_ref, r2b_ref, e2b_ref, bcn_ref, bce_ref, p_ref):
    tm, d = x_ref.shape
    j = pl.program_id(1)
    n_tb = tm // LANES
    n_sub = HEAD_DIM
    mod = mod_ref[0]

    @pl.when(j == 0)
    def _route():
        shift2 = mod[:, 3 * d:4 * d]
        scale2 = mod[:, 4 * d:5 * d]
        h2 = _rms(x_ref[...], n2w_ref[...]) * (1.0 + scale2) + shift2
        h2b = h2.astype(BF16)
        h2t_ref[...] = h2.T.astype(BF16)
        qt_ref[...] = _dot(h2b, wq_ref[...]).T.astype(BF16)
        acc_ref[...] = jnp.zeros_like(acc_ref)

        n_iota = lax.broadcasted_iota(jnp.int32, (n_sub, LANES), 0).astype(F32)
        head_iota = lax.broadcasted_iota(jnp.int32, (PEER_HEADS, LANES), 0)
        cands = _candidates()

        def tb_body(tb, carry):
            lanes = pl.ds(pl.multiple_of(tb * LANES, LANES), LANES)

            def extract(hp, hh, pp, break_ties):
                s = st_ref[hp]
                rk_ref[hp] = jnp.full((n_sub, LANES), float(PEER_TOPK), F32)
                for i in range(PEER_TOPK):
                    m = jnp.max(s, axis=0, keepdims=True)
                    if break_ties:
                        first = jnp.min(jnp.where(s == m, n_iota, float(n_sub)), axis=0, keepdims=True)
                        hit = n_iota == first
                    else:
                        hit = s == m
                    rk_ref[hp] = jnp.where(hit, float(i), rk_ref[hp])
                    s = jnp.where(hit, -jnp.inf, s)
                    vals_ref[pp, i] = jnp.where(head_iota == hh, m, vals_ref[pp, i])
                picked = jnp.sum(jnp.where(rk_ref[hp] < float(PEER_TOPK), 1.0, 0.0), axis=0, keepdims=True)
                return jnp.max(jnp.abs(picked - float(PEER_TOPK)))

            def hh_body(h2i, carry2):
                hps = [(2 * (2 * h2i + dh) + pp, 2 * h2i + dh, pp) for dh in range(2) for pp in range(2)]
                for hp, _, _ in hps:
                    qrows = pl.ds(pl.multiple_of(hp * HEAD_DIM, HEAD_DIM), HEAD_DIM)
                    st_ref[hp] = _dot(keys_ref[hp], qt_ref[qrows, lanes])
                excess = [extract(hp, hh, pp, False) for hp, hh, pp in hps]
                tied = functools.reduce(jnp.maximum, excess) > 0.0

                @pl.when(tied)
                def _():
                    for hp, hh, pp in hps:
                        extract(hp, hh, pp, True)

                return carry2

            lax.fori_loop(0, PEER_HEADS // 2, hh_body, 0)

            v1 = [vals_ref[0, i] for i in range(PEER_TOPK)]
            v2 = [vals_ref[1, i] for i in range(PEER_TOPK)]
            c = [v1[i] + v2[jj] for (i, jj) in cands]
            c00 = c[0]
            cnt = [jnp.zeros((PEER_HEADS, LANES), F32) for _ in range(PEER_TOPK)]
            z = jnp.zeros((PEER_HEADS, LANES), F32)
            big = float(PEER_TOPK * PEER_TOPK)
            for _ in range(PEER_TOPK):
                m = functools.reduce(jnp.maximum, c)
                first = functools.reduce(
                    jnp.minimum,
                    [jnp.where(c[k] == m, float(i * PEER_TOPK + jj), big) for k, (i, jj) in enumerate(cands)])
                z = z + jnp.exp(m - c00)
                for k, (i, jj) in enumerate(cands):
                    hit = first == float(i * PEER_TOPK + jj)
                    cnt[i] = cnt[i] + jnp.where(hit, 1.0, 0.0)
                    c[k] = jnp.where(hit, -jnp.inf, c[k])
            zinv = 1.0 / z

            for hh in range(PEER_HEADS):
                rank1 = rk_ref[2 * hh]
                n1 = jnp.zeros((n_sub, LANES), F32)
                for i in range(PEER_TOPK):
                    n1 = jnp.where(rank1 == float(i), cnt[i][hh:hh + 1, :], n1)
                n1f_ref[hh, tb] = n1
                e1f_ref[hh, tb] = jnp.exp(st_ref[2 * hh] - v1[0][hh:hh + 1, :])
                r2b_ref[hh, tb] = rk_ref[2 * hh + 1].astype(BF16)
                e2 = jnp.exp(st_ref[2 * hh + 1] - v2[0][hh:hh + 1, :]) * zinv[hh:hh + 1, :]
                e2b_ref[hh, tb] = e2.astype(BF16)
            return carry

        lax.fori_loop(0, n_tb, tb_body, 0)

    n_a = PEER_TE // n_sub
    assert n_a == SUBLANES
    arows = pl.ds(pl.multiple_of(j * n_a, SUBLANES), SUBLANES)
    pack = 2 * SUBLANES
    zero_b = jnp.zeros((pack, LANES), BF16)
    n_bv = n_sub // pack
    assert n_a % GATE_BLOCK_A == 0 and n_bv % GATE_BLOCK_B == 0

    for tb in range(n_tb):
        lanes = slice(tb * LANES, (tb + 1) * LANES)
        for hh in range(PEER_HEADS):
            n1t = n1f_ref[hh, tb, arows, :]
            e1t = e1f_ref[hh, tb, arows, :]
            for al in range(n_a):
                bcn_ref[tb, hh, al] = jnp.broadcast_to(n1t[al:al + 1, :], (pack, LANES)).astype(BF16)
                bce_ref[tb, hh, al] = jnp.broadcast_to(e1t[al:al + 1, :], (pack, LANES)).astype(BF16)

    nblk_a = n_a // GATE_BLOCK_A
    nblk_b = n_bv // GATE_BLOCK_B

    def gate_block(i, carry):
        tb = i // (nblk_a * nblk_b)
        a0 = ((i // nblk_b) % nblk_a) * GATE_BLOCK_A
        b0 = (i % nblk_b) * GATE_BLOCK_B
        lanes = pl.ds(pl.multiple_of(tb * LANES, LANES), LANES)
        w = [[zero_b for _ in range(GATE_BLOCK_B)] for _ in range(GATE_BLOCK_A)]
        for hh in range(PEER_HEADS):
            n1 = [bcn_ref[tb, hh, a0 + ia] for ia in range(GATE_BLOCK_A)]
            e1 = [bce_ref[tb, hh, a0 + ia] for ia in range(GATE_BLOCK_A)]
            for ib in range(GATE_BLOCK_B):
                rows = pl.ds(pl.multiple_of((b0 + ib) * pack, pack), pack)
                r2 = r2b_ref[hh, tb, rows, :]
                e2 = e2b_ref[hh, tb, rows, :]
                for ia in range(GATE_BLOCK_A):
                    w[ia][ib] = w[ia][ib] + jnp.where(r2 < n1[ia], e2, zero_b) * e1[ia]
        for ia in range(GATE_BLOCK_A):
            for ib in range(GATE_BLOCK_B):
                erows = pl.ds(pl.multiple_of((a0 + ia) * n_sub + (b0 + ib) * pack, pack), pack)
                p_ref[erows, lanes] = w[ia][ib]
        return carry

    lax.fori_loop(0, n_tb * nblk_a * nblk_b, gate_block, 0)

    for q in range(PEER_TE // ACT_CHUNK):
        rq = slice(q * ACT_CHUNK, (q + 1) * ACT_CHUNK)
        xa = _dot(u_ref[rq, :], h2t_ref[...])
        g = 0.5 * xa * (1.0 + lax.erf(xa * math.sqrt(0.5)))
        p_ref[rq, :] = g.astype(BF16) * p_ref[rq, :]
    acc_ref[...] += _dot(vt_ref[...], p_ref[...])

    @pl.when(j == pl.num_programs(1) - 1)
    def _finish():
        gate2 = mod[:, 5 * d:6 * d]
        x2 = x_ref[...] + gate2 * acc_ref[...].T
        o_ref[...] = _rms(x2, fw_ref[...])


def _peer(x1, mod3, seq, norm2_w, wq_b, keys_b, u_b, vt_b, final_w):
    t, d = x1.shape
    e = u_b.shape[0]
    qcols = wq_b.shape[1]
    tiles_per_seq = seq // PEER_TM
    n_tb = PEER_TM // LANES
    const2 = lambda i, j: (0, 0)
    return pl.pallas_call(
        _peer_kernel,
        grid=(t // PEER_TM, e // PEER_TE),
        in_specs=[
            pl.BlockSpec((PEER_TM, d), lambda i, j: (i, 0)),
            pl.BlockSpec((1, 1, mod3.shape[2]), lambda i, j: (i // tiles_per_seq, 0, 0)),
            pl.BlockSpec((1, d), const2),
            pl.BlockSpec((d, qcols), const2, pipeline_mode=pl.Buffered(1)),
            pl.BlockSpec(keys_b.shape, lambda i, j: (0, 0, 0), pipeline_mode=pl.Buffered(1)),
            pl.BlockSpec((PEER_TE, d), lambda i, j: (j, 0)),
            pl.BlockSpec((d, PEER_TE), lambda i, j: (0, j)),
            pl.BlockSpec((1, d), const2),
        ],
        out_specs=pl.BlockSpec((PEER_TM, d), lambda i, j: (i, 0)),
        out_shape=jax.ShapeDtypeStruct((t, d), F32),
        scratch_shapes=[
            pltpu.VMEM((d, PEER_TM), BF16),
            pltpu.VMEM((qcols, PEER_TM), BF16),
            pltpu.VMEM((d, PEER_TM), F32),
            pltpu.VMEM((2 * PEER_HEADS, HEAD_DIM, LANES), F32),
            pltpu.VMEM((2 * PEER_HEADS, HEAD_DIM, LANES), F32),
            pltpu.VMEM((2, PEER_TOPK, PEER_HEADS, LANES), F32),
            pltpu.VMEM((PEER_HEADS, n_tb, HEAD_DIM, LANES), F32),
            pltpu.VMEM((PEER_HEADS, n_tb, HEAD_DIM, LANES), F32),
            pltpu.VMEM((PEER_HEADS, n_tb, HEAD_DIM, LANES), BF16),
            pltpu.VMEM((PEER_HEADS, n_tb, HEAD_DIM, LANES), BF16),
            pltpu.VMEM((n_tb, PEER_HEADS, SUBLANES, 2 * SUBLANES, LANES), BF16),
            pltpu.VMEM((n_tb, PEER_HEADS, SUBLANES, 2 * SUBLANES, LANES), BF16),
            pltpu.VMEM((PEER_TE, PEER_TM), BF16),
        ],
        compiler_params=pltpu.CompilerParams(
            dimension_semantics=("arbitrary", "arbitrary"), vmem_limit_bytes=VMEM_LIMIT_BYTES),
        name="peer_dense",
    )(x1, mod3, norm2_w, wq_b, keys_b, u_b, vt_b, final_w)


def kernel(x, c, w_ada, b_ada, norm1_w, w_in, hgrn_lb_logits, hgrn_onorm_w, conv_w, conv_onorm_w,
           w_out, norm2_w, peer_w_query, peer_sub_keys, peer_u, peer_v, final_norm_w):
    bsz, seq, d = x.shape
    assert w_ada.shape[0] == 1, "single-layer kernel"
    assert seq % MIX_TS == 0 and seq % PEER_TM == 0 and MIX_TS % CHUNK == 0
    assert peer_u.shape[1] % PEER_TE == 0 and PEER_TE % HEAD_DIM == 0

    c_pad = jnp.pad(c, ((0, SUBLANES - bsz), (0, 0)))
    mod = _modulation(c_pad, w_ada[0], b_ada[0][None, :])
    mod3 = mod[:bsz].reshape(bsz, 1, mod.shape[1])

    u_b, vt_b = _prep_tables(peer_u[0], peer_v[0])

    x1 = _mixer(x, mod3, norm1_w[0][None, :], w_in[0].astype(BF16), hgrn_lb_logits,
                hgrn_onorm_w[0][None, :], conv_w[0], conv_onorm_w[0][None, :], w_out[0].astype(BF16))

    keys_b = peer_sub_keys[0].reshape(2 * PEER_HEADS, HEAD_DIM, HEAD_DIM).astype(BF16)
    out = _peer(x1.reshape(bsz * seq, d), mod3, seq, norm2_w[0][None, :],
                peer_w_query[0].astype(BF16), keys_b, u_b, vt_b, final_norm_w[None, :])
    return out.reshape(bsz, seq, d)
```
